```python
import math
import jax, jax.numpy as jnp
from jax import lax
import numpy as np

D_MODEL = 1024
BATCH = 16
SEQ = 2048
DEPTH = 1

N_META = 16
Q_BLOCK = 128
N_DIFF_HEADS = 4
DIFF_QK_DIM = 64
DIFF_V_DIM = 2 * DIFF_QK_DIM
DIFF_WIDTH = N_DIFF_HEADS * DIFF_V_DIM
N_SB_HEADS = 8
SB_HEAD_DIM = 64
SB_WIDTH = N_SB_HEADS * SB_HEAD_DIM
MIX_WIDTH = DIFF_WIDTH + SB_WIDTH
PROJ_SIZES = (
    N_DIFF_HEADS * 2 * DIFF_QK_DIM,
    N_DIFF_HEADS * 2 * DIFF_QK_DIM,
    N_DIFF_HEADS * DIFF_V_DIM,
    N_SB_HEADS * SB_HEAD_DIM,
    N_SB_HEADS * SB_HEAD_DIM,
    N_SB_HEADS * SB_HEAD_DIM,
)
PROJ_WIDTH = sum(PROJ_SIZES)
D_FF = 2816
CONV_WIDTH = 3
EPS = 1e-6

kernel_name = "hymba_diffattn_stickbreaking_convffn"


def _rmsnorm(x, g):
    xf = x.astype(jnp.float32)
    xf = xf * lax.rsqrt(jnp.mean(xf * xf, axis=-1, keepdims=True) + EPS)
    return xf.astype(x.dtype) * g


def _alibi_slopes(n_heads):
    start = 2.0 ** (-8.0 / n_heads)
    return jnp.asarray([start ** (i + 1) for i in range(n_heads)], dtype=jnp.float32)


def _lambda_init(layer_idx):
    return 0.8 - 0.6 * math.exp(-0.3 * layer_idx)


def _attend_block(qd1, qd2, qs, qpos, kd1, kd2, vd, ks, vs, kpos, slopes, lam):
    dist = qpos[:, None] - kpos[None, :]
    causal = dist >= 0
    alibi = -slopes[:, None, None] * dist.astype(jnp.float32)[None]

    def softmax_map(q, k):
        s = jnp.einsum("bhtd,bhsd->bhts", q, k).astype(jnp.float32) * (DIFF_QK_DIM ** -0.5) + alibi
        return jax.nn.softmax(jnp.where(causal, s, -jnp.inf), axis=-1)

    a_diff = softmax_map(qd1, kd1) - lam * softmax_map(qd2, kd2)
    o_diff = jnp.einsum("bhts,bhse->bhte", a_diff.astype(vd.dtype), vd)
    strict = dist > 0
    z = jnp.einsum("bhtd,bhsd->bhts", qs, ks).astype(jnp.float32) * (SB_HEAD_DIM ** -0.5)
    sp = jnp.where(strict, jax.nn.softplus(z), 0.0)
    log_a = z - lax.cumsum(sp, axis=3, reverse=True)
    a_sb = jnp.exp(jnp.where(strict, log_a, -jnp.inf))
    o_sb = jnp.einsum("bhts,bhse->bhte", a_sb.astype(vs.dtype), vs)
    return o_diff, o_sb


def _token_mixer(xn, w_in, lam_q1, lam_k1, lam_q2, lam_k2, g_diff, g_sb, w_out, layer_idx):
    B, L, _ = xn.shape
    n_blocks = (L - N_META) // Q_BLOCK
    proj = xn @ w_in
    dq, dk, dv, sq, sk, sv = jnp.split(proj, list(np.cumsum(PROJ_SIZES)[:-1]), axis=-1)

    def heads(t, h, d):
        return t.reshape(B, L, h, d).transpose(0, 2, 1, 3)

    dq = dq.reshape(B, L, N_DIFF_HEADS, 2, DIFF_QK_DIM)
    dk = dk.reshape(B, L, N_DIFF_HEADS, 2, DIFF_QK_DIM)
    qd1, qd2 = dq[..., 0, :].transpose(0, 2, 1, 3), dq[..., 1, :].transpose(0, 2, 1, 3)
    kd1, kd2 = dk[..., 0, :].transpose(0, 2, 1, 3), dk[..., 1, :].transpose(0, 2, 1, 3)
    vd = heads(dv, N_DIFF_HEADS, DIFF_V_DIM)
    qs, ks, vs = heads(sq, N_SB_HEADS, SB_HEAD_DIM), heads(sk, N_SB_HEADS, SB_HEAD_DIM), heads(sv, N_SB_HEADS, SB_HEAD_DIM)

    lam_init = _lambda_init(layer_idx)
    lam = (jnp.exp(jnp.sum(lam_q1.astype(jnp.float32) * lam_k1.astype(jnp.float32)))
           - jnp.exp(jnp.sum(lam_q2.astype(jnp.float32) * lam_k2.astype(jnp.float32))) + lam_init)
    slopes = _alibi_slopes(N_DIFF_HEADS)
    pos = jnp.arange(L, dtype=jnp.int32)

    m = N_META
    od_meta, os_meta = _attend_block(qd1[:, :, :m], qd2[:, :, :m], qs[:, :, :m], pos[:m],
                                     kd1[:, :, :m], kd2[:, :, :m], vd[:, :, :m],
                                     ks[:, :, :m], vs[:, :, :m], pos[:m], slopes, lam)

    def to_blocks(t):
        h, d = t.shape[1], t.shape[3]
        return t[:, :, m:].reshape(B, h, n_blocks, Q_BLOCK, d).transpose(2, 0, 1, 3, 4)

    def from_blocks(t):
        return t.transpose(1, 2, 0, 3, 4).reshape(B, t.shape[2], n_blocks * Q_BLOCK, t.shape[4])

    qpos_blocks = pos[m:].reshape(n_blocks, Q_BLOCK)
    od_real, os_real = lax.map(
        lambda a: _attend_block(a[0], a[1], a[2], a[3], kd1, kd2, vd, ks, vs, pos, slopes, lam),
        (to_blocks(qd1), to_blocks(qd2), to_blocks(qs), qpos_blocks))
    o_diff = jnp.concatenate([od_meta, from_blocks(od_real)], axis=2)
    o_sb = jnp.concatenate([os_meta, from_blocks(os_real)], axis=2)

    o_diff = _rmsnorm(o_diff, g_diff) * (1.0 - lam_init)
    o_sb = _rmsnorm(o_sb, g_sb)
    merged = jnp.concatenate([o_diff.transpose(0, 2, 1, 3).reshape(B, L, DIFF_WIDTH),
                              o_sb.transpose(0, 2, 1, 3).reshape(B, L, SB_WIDTH)], axis=-1)
    return merged @ w_out


def _conv_ffn(xn, w_up, conv_w, conv_b, w_down):
    L = xn.shape[1]
    h = xn @ w_up
    hp = jnp.pad(h, ((0, 0), (CONV_WIDTH - 1, 0), (0, 0)))
    hc = conv_b + sum(conv_w[k] * hp[:, k:k + L] for k in range(CONV_WIDTH))
    gate, up = jnp.split(hc, 2, axis=-1)
    return (jax.nn.silu(gate) * up) @ w_down


def setup_inputs(seed: int = 0) -> dict:
    key = jax.random.key(seed)
    ks = jax.random.split(key, 16)
    f32 = jnp.float32
    nrm = lambda k, shape, s: jax.random.normal(k, shape, f32) * s
    return {
        "x": nrm(ks[0], (BATCH, SEQ, D_MODEL), 1.0),
        "meta_tokens": nrm(ks[1], (N_META, D_MODEL), 1.0),
        "g_attn": 1.0 + nrm(ks[2], (DEPTH, D_MODEL), 0.02),
        "w_in": nrm(ks[3], (DEPTH, D_MODEL, PROJ_WIDTH), D_MODEL ** -0.5),
        "lam_q1": nrm(ks[4], (DEPTH, DIFF_QK_DIM), 0.1),
        "lam_k1": nrm(ks[5], (DEPTH, DIFF_QK_DIM), 0.1),
        "lam_q2": nrm(ks[6], (DEPTH, DIFF_QK_DIM), 0.1),
        "lam_k2": nrm(ks[7], (DEPTH, DIFF_QK_DIM), 0.1),
        "g_diff": 1.0 + nrm(ks[8], (DEPTH, DIFF_V_DIM), 0.02),
        "g_sb": 1.0 + nrm(ks[9], (DEPTH, SB_HEAD_DIM), 0.02),
        "w_out": nrm(ks[10], (DEPTH, MIX_WIDTH, D_MODEL), MIX_WIDTH ** -0.5),
        "g_ffn": 1.0 + nrm(ks[11], (DEPTH, D_MODEL), 0.02),
        "w_up": nrm(ks[12], (DEPTH, D_MODEL, 2 * D_FF), D_MODEL ** -0.5),
        "conv_w": nrm(ks[13], (DEPTH, CONV_WIDTH, 2 * D_FF), CONV_WIDTH ** -0.5),
        "conv_b": nrm(ks[14], (DEPTH, 2 * D_FF), 0.01),
        "w_down": nrm(ks[15], (DEPTH, D_FF, D_MODEL), D_FF ** -0.5),
        "g_final": 1.0 + nrm(jax.random.fold_in(key, 99), (D_MODEL,), 0.02),
    }


def reference(x, meta_tokens, g_attn, w_in, lam_q1, lam_k1, lam_q2, lam_k2, g_diff, g_sb,
              w_out, g_ffn, w_up, conv_w, conv_b, w_down, g_final):
    B = x.shape[0]
    meta = jnp.broadcast_to(meta_tokens[None].astype(x.dtype), (B, N_META, D_MODEL))
    h = jnp.concatenate([meta, x], axis=1)
    for layer in range(DEPTH):
        h = h + _token_mixer(_rmsnorm(h, g_attn[layer]), w_in[layer], lam_q1[layer], lam_k1[layer],
                             lam_q2[layer], lam_k2[layer], g_diff[layer], g_sb[layer],
                             w_out[layer], layer)
        h = h + _conv_ffn(_rmsnorm(h, g_ffn[layer]), w_up[layer], conv_w[layer], conv_b[layer],
                          w_down[layer])
    return _rmsnorm(h, g_final)[:, N_META:]
```

```python
import functools
import math

import jax
import jax.numpy as jnp
from jax import lax
from jax.experimental import pallas as pl
from jax.experimental.pallas import tpu as pltpu

D_MODEL = 1024
N_META = 16
N_DIFF_HEADS = 4
DIFF_QK_DIM = 64
DIFF_V_DIM = 128
N_SB_HEADS = 8
SB_HEAD_DIM = 64
GROUP_WIDTH = 512
PROJ_WIDTH = 6 * GROUP_WIDTH
D_FF = 2816
EPS = 1e-6
LAMBDA_INIT = 0.8 - 0.6 * math.exp(-0.3 * 0)
NEG = -1e30

LANES = 128
META_ROWS = 128
TQ = 256
TK = 256
FF_CHUNK = 256
HALO_ROWS = 8

VMEM_LIMIT = 56 * 1024 * 1024

_NT = (((1,), (1,)), ((), ()))


def _alibi_slope(h):
    start = 2.0 ** (-8.0 / N_DIFF_HEADS)
    return start ** (h + 1)


def _params(n_grid):
    return pltpu.CompilerParams(dimension_semantics=("arbitrary",) * n_grid,
                                vmem_limit_bytes=VMEM_LIMIT)


def _rms_scale(x):
    return lax.rsqrt(jnp.mean(x * x, axis=-1, keepdims=True) + EPS)


def _inproj_kernel(x_ref, g_ref, w_ref, o_ref):
    x = x_ref[...]
    xn = (x * _rms_scale(x) * g_ref[...]).astype(jnp.bfloat16)
    for c in range(PROJ_WIDTH // GROUP_WIDTH):
        cols = slice(c * GROUP_WIDTH, (c + 1) * GROUP_WIDTH)
        o_ref[:, cols] = jnp.dot(xn, w_ref[:, cols],
                                 preferred_element_type=jnp.float32).astype(jnp.bfloat16)


def _inproj(rows, g, w, tm):
    n = rows.shape[0]
    return pl.pallas_call(
        _inproj_kernel,
        grid=(n // tm,),
        in_specs=[pl.BlockSpec((tm, D_MODEL), lambda i: (i, 0)),
                  pl.BlockSpec((1, D_MODEL), lambda i: (0, 0)),
                  pl.BlockSpec((D_MODEL, PROJ_WIDTH), lambda i: (0, 0))],
        out_specs=pl.BlockSpec((tm, PROJ_WIDTH), lambda i: (i, 0)),
        out_shape=jax.ShapeDtypeStruct((n, PROJ_WIDTH), jnp.bfloat16),
        compiler_params=_params(1),
        name="inproj",
    )(rows, g, w)


def _lambda_value(lam_ref):
    lam = lam_ref[...]
    s1 = jnp.sum(lam[0:1] * lam[1:2], axis=-1, keepdims=True)
    s2 = jnp.sum(lam[2:3] * lam[3:4], axis=-1, keepdims=True)
    return jnp.exp(s1) - jnp.exp(s2) + LAMBDA_INIT


def _softplus(z):
    return jnp.maximum(z, 0.0) + jnp.log(1.0 + jnp.exp(-jnp.abs(z)))


def _diff_step(q_pair, k, v, bias, mask, m_ref, l_ref, acc_ref):
    for which, q in enumerate(q_pair):
        s = lax.dot_general(q, k, _NT, preferred_element_type=jnp.float32) + bias
        if mask is not None:
            s = jnp.where(mask, s, NEG)
        m_old = m_ref[which]
        m_new = jnp.maximum(m_old, jnp.max(s, axis=-1, keepdims=True))
        alpha = jnp.exp(m_old - m_new)
        p = jnp.exp(s - m_new)
        l_ref[which] = alpha * l_ref[which] + jnp.sum(p, axis=-1, keepdims=True)
        acc_ref[which] = alpha * acc_ref[which] + jnp.dot(
            p.astype(jnp.bfloat16), v, preferred_element_type=jnp.float32)
        m_ref[which] = m_new


def _sb_step(q, k, v, tri, mask, c_ref, acc_ref):
    z = lax.dot_general(q, k, _NT, preferred_element_type=jnp.float32)
    sp = _softplus(z)
    if mask is not None:
        sp = jnp.where(mask, sp, 0.0)
    hi = sp.astype(jnp.bfloat16)
    lo = (sp - hi.astype(jnp.float32)).astype(jnp.bfloat16)
    cum = (jnp.dot(hi, tri, preferred_element_type=jnp.float32)
           + jnp.dot(lo, tri, preferred_element_type=jnp.float32))
    a = jnp.exp(z - cum - c_ref[...])
    if mask is not None:
        a = jnp.where(mask, a, 0.0)
    acc_ref[...] += jnp.dot(a.astype(jnp.bfloat16), v, preferred_element_type=jnp.float32)
    c_ref[...] += jnp.sum(sp, axis=-1, keepdims=True)


def _attn_body(lam_ref, gd_ref, gs_ref, qd_ref, qs_ref, kd_ref, vd_ref, ks_ref, vs_ref,
               kdm_ref, vdm_ref, ksm_ref, vsm_ref, o_ref,
               m_ref, l_ref, acc_ref, c_ref, sacc_ref, tri_ref, *, tq, meta_only):
    j = 0 if meta_only else pl.program_id(1)
    lam = _lambda_value(lam_ref)
    lane = lax.broadcasted_iota(jnp.int32, (1, LANES), 1)
    lo_lanes = lane < DIFF_QK_DIM
    row = lax.broadcasted_iota(jnp.int32, (tq, TK), 0)
    col = lax.broadcasted_iota(jnp.int32, (tq, TK), 1)
    mrow = lax.broadcasted_iota(jnp.int32, (tq, META_ROWS), 0)
    mcol = lax.broadcasted_iota(jnp.int32, (tq, META_ROWS), 1)
    meta_valid = mcol < N_META
    tri_r = lax.broadcasted_iota(jnp.int32, (TK, TK), 0)
    tri_c = lax.broadcasted_iota(jnp.int32, (TK, TK), 1)
    tri_ref[...] = jnp.where(tri_r >= tri_c, 1.0, 0.0).astype(jnp.bfloat16)

    q0 = 0 if meta_only else N_META + j * tq
    meta_pos = lax.broadcasted_iota(jnp.int32, (1, META_ROWS), 1)
    blk_pos = lax.broadcasted_iota(jnp.int32, (1, TK), 1)

    for h in range(N_DIFF_HEADS):
        cols = slice(h * LANES, (h + 1) * LANES)
        slope = _alibi_slope(h)
        q = qd_ref[0, :, cols]
        zero = jnp.zeros_like(q)
        q_pair = (jnp.where(lo_lanes, q, zero), jnp.where(lo_lanes, zero, q))
        m_ref[...] = jnp.full(m_ref.shape, NEG, jnp.float32)
        l_ref[...] = jnp.zeros(l_ref.shape, jnp.float32)
        acc_ref[...] = jnp.zeros(acc_ref.shape, jnp.float32)

        meta_bias = slope * (meta_pos - q0).astype(jnp.float32)
        meta_mask = (meta_valid & (mcol <= mrow)) if meta_only else meta_valid
        _diff_step(q_pair, kdm_ref[:, cols], vdm_ref[:, cols], meta_bias, meta_mask,
                   m_ref, l_ref, acc_ref)
        if not meta_only:
            def diff_mid(i, carry, q_pair=q_pair, cols=cols, slope=slope):
                start = pl.multiple_of(i * TK, TK)
                bias = slope * ((i - j) * TK + blk_pos).astype(jnp.float32)
                _diff_step(q_pair, kd_ref[0, pl.ds(start, TK), cols],
                           vd_ref[0, pl.ds(start, TK), cols], bias, None, m_ref, l_ref, acc_ref)
                return carry
            lax.fori_loop(0, j, diff_mid, 0)
            start = pl.multiple_of(j * TK, TK)
            _diff_step(q_pair, kd_ref[0, pl.ds(start, TK), cols],
                       vd_ref[0, pl.ds(start, TK), cols],
                       slope * blk_pos.astype(jnp.float32), col <= row, m_ref, l_ref, acc_ref)

        o = acc_ref[0] / l_ref[0] - lam * (acc_ref[1] / l_ref[1])
        o = o * _rms_scale(o) * gd_ref[...] * (1.0 - LAMBDA_INIT)
        o_ref[0, :, cols] = o.astype(o_ref.dtype)

    for hp in range(N_SB_HEADS // 2):
        cols = slice(hp * LANES, (hp + 1) * LANES)
        q = qs_ref[0, :, cols]
        zero = jnp.zeros_like(q)
        for half in range(2):
            qh = jnp.where(lo_lanes, q, zero) if half == 0 else jnp.where(lo_lanes, zero, q)
            c_ref[...] = jnp.zeros(c_ref.shape, jnp.float32)
            sacc_ref[half] = jnp.zeros(sacc_ref.shape[1:], jnp.float32)
            acc_view = sacc_ref.at[half]
            if not meta_only:
                start = pl.multiple_of(j * TK, TK)
                _sb_step(qh, ks_ref[0, pl.ds(start, TK), cols], vs_ref[0, pl.ds(start, TK), cols],
                         tri_ref[...], col < row, c_ref, acc_view)

                def sb_mid(t, carry, qh=qh, cols=cols, acc_view=acc_view):
                    start = pl.multiple_of((j - 1 - t) * TK, TK)
                    _sb_step(qh, ks_ref[0, pl.ds(start, TK), cols],
                             vs_ref[0, pl.ds(start, TK), cols], tri_ref[...], None, c_ref, acc_view)
                    return carry
                lax.fori_loop(0, j, sb_mid, 0)
            meta_mask = (meta_valid & (mcol < mrow)) if meta_only else meta_valid
            _sb_step(qh, ksm_ref[:, cols], vsm_ref[:, cols],
                     tri_ref[0:META_ROWS, 0:META_ROWS], meta_mask, c_ref, acc_view)
        o = jnp.where(lo_lanes, sacc_ref[0], sacc_ref[1])
        sq = o * o
        ms_lo = jnp.sum(jnp.where(lo_lanes, sq, 0.0), axis=-1, keepdims=True)
        ms_hi = jnp.sum(jnp.where(lo_lanes, 0.0, sq), axis=-1, keepdims=True)
        ms = jnp.where(lo_lanes, ms_lo, ms_hi) * (1.0 / SB_HEAD_DIM)
        o = o * lax.rsqrt(ms + EPS) * gs_ref[...]
        ocols = slice(GROUP_WIDTH + hp * LANES, GROUP_WIDTH + (hp + 1) * LANES)
        o_ref[0, :, ocols] = o.astype(o_ref.dtype)


def _attn_scratch(tq):
    return [pltpu.VMEM((2, tq, 1), jnp.float32),
            pltpu.VMEM((2, tq, 1), jnp.float32),
            pltpu.VMEM((2, tq, LANES), jnp.float32),
            pltpu.VMEM((tq, 1), jnp.float32),
            pltpu.VMEM((2, tq, LANES), jnp.float32),
            pltpu.VMEM((TK, TK), jnp.bfloat16)]


def _meta_kv_specs(index_map_for):
    return [pl.BlockSpec((META_ROWS, GROUP_WIDTH), index_map_for(g)) for g in (1, 2, 4, 5)]


def _attn(proj, proj_meta, lam, gd, gs, batch, seq):
    proj3 = proj.reshape(batch, seq, PROJ_WIDTH)
    small = [pl.BlockSpec(lam.shape, lambda b, j: (0, 0)),
             pl.BlockSpec(gd.shape, lambda b, j: (0, 0)),
             pl.BlockSpec(gs.shape, lambda b, j: (0, 0))]
    qspec = lambda g: pl.BlockSpec((1, TQ, GROUP_WIDTH), lambda b, j, g=g: (b, j, g))
    kvspec = lambda g: pl.BlockSpec((1, seq, GROUP_WIDTH), lambda b, j, g=g: (b, 0, g))
    meta_specs = _meta_kv_specs(lambda g: (lambda b, j, g=g: (0, g)))
    out = pl.pallas_call(
        functools.partial(_attn_body, tq=TQ, meta_only=False),
        grid=(batch, seq // TQ),
        in_specs=small + [qspec(0), qspec(3), kvspec(1), kvspec(2), kvspec(4), kvspec(5)] + meta_specs,
        out_specs=pl.BlockSpec((1, TQ, D_MODEL), lambda b, j: (b, j, 0)),
        out_shape=jax.ShapeDtypeStruct((batch, seq, D_MODEL), jnp.bfloat16),
        scratch_shapes=_attn_scratch(TQ),
        compiler_params=_params(2),
        name="attn",
    )(lam, gd, gs, proj3, proj3, proj3, proj3, proj3, proj3,
      proj_meta, proj_meta, proj_meta, proj_meta)
    return out.reshape(batch * seq, D_MODEL)


def _attn_meta_body(lam_ref, gd_ref, gs_ref, qd_ref, qs_ref, kdm_ref, vdm_ref, ksm_ref, vsm_ref,
                    o_ref, *scratch):
    _attn_body(lam_ref, gd_ref, gs_ref, qd_ref, qs_ref, None, None, None, None,
               kdm_ref, vdm_ref, ksm_ref, vsm_ref, o_ref, *scratch,
               tq=META_ROWS, meta_only=True)


def _attn_meta(proj_meta, lam, gd, gs):
    proj3 = proj_meta.reshape(1, META_ROWS, PROJ_WIDTH)
    small = [pl.BlockSpec(lam.shape, lambda i: (0, 0)),
             pl.BlockSpec(gd.shape, lambda i: (0, 0)),
             pl.BlockSpec(gs.shape, lambda i: (0, 0))]
    qspec = lambda g: pl.BlockSpec((1, META_ROWS, GROUP_WIDTH), lambda i, g=g: (0, 0, g))
    meta_specs = _meta_kv_specs(lambda g: (lambda i, g=g: (0, g)))
    out = pl.pallas_call(
        _attn_meta_body,
        grid=(1,),
        in_specs=small + [qspec(0), qspec(3)] + meta_specs,
        out_specs=pl.BlockSpec((1, META_ROWS, D_MODEL), lambda i: (0, 0, 0)),
        out_shape=jax.ShapeDtypeStruct((1, META_ROWS, D_MODEL), jnp.bfloat16),
        scratch_shapes=_attn_scratch(META_ROWS),
        compiler_params=_params(1),
        name="attn_meta",
    )(lam, gd, gs, proj3, proj3, proj_meta, proj_meta, proj_meta, proj_meta)
    return out.reshape(META_ROWS, D_MODEL)


def _outproj_kernel(x_ref, a_ref, w_ref, o_ref):
    o_ref[...] = x_ref[...] + jnp.dot(a_ref[...], w_ref[...], preferred_element_type=jnp.float32)


def _outproj(x_rows, merged, w, tm):
    n = x_rows.shape[0]
    return pl.pallas_call(
        _outproj_kernel,
        grid=(n // tm,),
        in_specs=[pl.BlockSpec((tm, D_MODEL), lambda i: (i, 0)),
                  pl.BlockSpec((tm, D_MODEL), lambda i: (i, 0)),
                  pl.BlockSpec((D_MODEL, D_MODEL), lambda i: (0, 0))],
        out_specs=pl.BlockSpec((tm, D_MODEL), lambda i: (i, 0)),
        out_shape=jax.ShapeDtypeStruct((n, D_MODEL), jnp.float32),
        compiler_params=_params(1),
        name="outproj",
    )(x_rows, merged, w)


def _meta_up_kernel(h_ref, g_ref, w_ref, o_ref):
    h = h_ref[...]
    xn = (h * _rms_scale(h) * g_ref[...]).astype(jnp.bfloat16)
    for c in range(2 * D_FF // FF_CHUNK):
        cols = slice(c * FF_CHUNK, (c + 1) * FF_CHUNK)
        o_ref[:, cols] = jnp.dot(xn, w_ref[:, cols], preferred_element_type=jnp.float32)


def _meta_up(h_meta, g, w):
    return pl.pallas_call(
        _meta_up_kernel,
        grid=(1,),
        in_specs=[pl.BlockSpec((META_ROWS, D_MODEL), lambda i: (0, 0)),
                  pl.BlockSpec((1, D_MODEL), lambda i: (0, 0)),
                  pl.BlockSpec((D_MODEL, 2 * D_FF), lambda i: (0, 0))],
        out_specs=pl.BlockSpec((META_ROWS, 2 * D_FF), lambda i: (0, 0)),
        out_shape=jax.ShapeDtypeStruct((META_ROWS, 2 * D_FF), jnp.float32),
        compiler_params=_params(1),
        name="meta_up",
    )(h_meta, g, w)


def _ffn_up_kernel(h_ref, g_ref, w_ref, cw_ref, cb_ref, halo0_ref, o_ref, halo_ref, *, tiles_per_seq):
    i = pl.program_id(0)

    @pl.when(i % tiles_per_seq == 0)
    def _():
        halo_ref[...] = halo0_ref[...]

    tm = h_ref.shape[0]
    h = h_ref[...]
    xn = (h * _rms_scale(h) * g_ref[...]).astype(jnp.bfloat16)
    row = lax.broadcasted_iota(jnp.int32, (tm, FF_CHUNK), 0)

    def conv(cols):
        u = jnp.dot(xn, w_ref[:, cols], preferred_element_type=jnp.float32)
        prev2 = halo_ref[HALO_ROWS - 2:HALO_ROWS - 1, cols]
        prev1 = halo_ref[HALO_ROWS - 1:HALO_ROWS, cols]
        u1 = jnp.where(row == 0, prev1, pltpu.roll(u, 1, 0))
        u2 = jnp.where(row == 0, prev2, jnp.where(row == 1, prev1, pltpu.roll(u, 2, 0)))
        halo_ref[:, cols] = u[tm - HALO_ROWS:, :]
        return cb_ref[:, cols] + cw_ref[0:1, cols] * u2 + cw_ref[1:2, cols] * u1 + cw_ref[2:3, cols] * u

    for c in range(D_FF // FF_CHUNK):
        gate = conv(slice(c * FF_CHUNK, (c + 1) * FF_CHUNK))
        up = conv(slice(D_FF + c * FF_CHUNK, D_FF + (c + 1) * FF_CHUNK))
        act = gate * (1.0 / (1.0 + jnp.exp(-gate))) * up
        o_ref[:, c * FF_CHUNK:(c + 1) * FF_CHUNK] = act.astype(o_ref.dtype)


def _ffn_up(h1, g, w, cw, cb, halo0, tm, seq):
    n = h1.shape[0]
    return pl.pallas_call(
        functools.partial(_ffn_up_kernel, tiles_per_seq=seq // tm),
        grid=(n // tm,),
        in_specs=[pl.BlockSpec((tm, D_MODEL), lambda i: (i, 0)),
                  pl.BlockSpec((1, D_MODEL), lambda i: (0, 0)),
                  pl.BlockSpec((D_MODEL, 2 * D_FF), lambda i: (0, 0)),
                  pl.BlockSpec((3, 2 * D_FF), lambda i: (0, 0)),
                  pl.BlockSpec((1, 2 * D_FF), lambda i: (0, 0)),
                  pl.BlockSpec((HALO_ROWS, 2 * D_FF), lambda i: (0, 0))],
        out_specs=pl.BlockSpec((tm, D_FF), lambda i: (i, 0)),
        out_shape=jax.ShapeDtypeStruct((n, D_FF), jnp.bfloat16),
        scratch_shapes=[pltpu.VMEM((HALO_ROWS, 2 * D_FF), jnp.float32)],
        compiler_params=_params(1),
        name="ffn_up",
    )(h1, g, w, cw, cb, halo0)


def _ffn_down_kernel(h_ref, a_ref, w_ref, g_ref, o_ref):
    y = h_ref[...] + jnp.dot(a_ref[...], w_ref[...], preferred_element_type=jnp.float32)
    o_ref[...] = y * _rms_scale(y) * g_ref[...]


def _ffn_down(h1, act, w, g, tm):
    n = h1.shape[0]
    return pl.pallas_call(
        _ffn_down_kernel,
        grid=(n // tm,),
        in_specs=[pl.BlockSpec((tm, D_MODEL), lambda i: (i, 0)),
                  pl.BlockSpec((tm, D_FF), lambda i: (i, 0)),
                  pl.BlockSpec((D_FF, D_MODEL), lambda i: (0, 0)),
                  pl.BlockSpec((1, D_MODEL), lambda i: (0, 0))],
        out_specs=pl.BlockSpec((tm, D_MODEL), lambda i: (i, 0)),
        out_shape=jax.ShapeDtypeStruct((n, D_MODEL), jnp.float32),
        compiler_params=_params(1),
        name="ffn_down",
    )(h1, act, w, g)


def kernel(x, meta_tokens, g_attn, w_in, lam_q1, lam_k1, lam_q2, lam_k2, g_diff, g_sb, w_out, g_ffn,
           w_up, conv_w, conv_b, w_down, g_final):
    batch, seq, _ = x.shape
    bf16 = jnp.bfloat16
    qscale = jnp.concatenate([jnp.full((GROUP_WIDTH,), DIFF_QK_DIM ** -0.5, jnp.float32),
                              jnp.ones((2 * GROUP_WIDTH,), jnp.float32),
                              jnp.full((GROUP_WIDTH,), SB_HEAD_DIM ** -0.5, jnp.float32),
                              jnp.ones((2 * GROUP_WIDTH,), jnp.float32)])
    w_in_b = (w_in[0] * qscale).astype(bf16)
    w_out_b = w_out[0].astype(bf16)
    w_up_b = w_up[0].astype(bf16)
    w_down_b = w_down[0].astype(bf16)
    lam = jnp.stack([lam_q1[0], lam_k1[0], lam_q2[0], lam_k2[0]]).astype(jnp.float32)
    gd = g_diff[0].reshape(1, DIFF_V_DIM)
    gs = jnp.tile(g_sb[0], 2).reshape(1, LANES)
    g_attn2, g_ffn2, g_final2 = g_attn[0].reshape(1, -1), g_ffn[0].reshape(1, -1), g_final.reshape(1, -1)

    x_rows = x.reshape(batch * seq, D_MODEL)
    meta_rows = jnp.pad(meta_tokens.astype(x.dtype), ((0, META_ROWS - N_META), (0, 0)))

    proj_meta = _inproj(meta_rows, g_attn2, w_in_b, META_ROWS)
    merged_meta = _attn_meta(proj_meta, lam, gd, gs)
    h1_meta = _outproj(meta_rows, merged_meta, w_out_b, META_ROWS)
    u_meta = _meta_up(h1_meta, g_ffn2, w_up_b)
    halo0 = jnp.pad(u_meta[N_META - 2:N_META], ((HALO_ROWS - 2, 0), (0, 0)))

    proj = _inproj(x_rows, g_attn2, w_in_b, 512)
    merged = _attn(proj, proj_meta, lam, gd, gs, batch, seq)
    h1 = _outproj(x_rows, merged, w_out_b, 512)
    act = _ffn_up(h1, g_ffn2, w_up_b, conv_w[0], conv_b[0].reshape(1, -1), halo0, 256, seq)
    out = _ffn_down(h1, act, w_down_b, g_final2, 512)
    return out.reshape(batch, seq, D_MODEL)
```

```python
import functools
import math

import jax
import jax.numpy as jnp
from jax import lax
from jax.experimental import pallas as pl
from jax.experimental.pallas import tpu as pltpu

D_MODEL = 1024
N_META = 16
N_DIFF_HEADS = 4
DIFF_QK_DIM = 64
DIFF_V_DIM = 128
N_SB_HEADS = 8
SB_HEAD_DIM = 64
GROUP_WIDTH = 512
PROJ_WIDTH = 6 * GROUP_WIDTH
D_FF = 2816
EPS = 1e-6
LAMBDA_INIT = 0.8 - 0.6 * math.exp(-0.3 * 0)
NEG = -1e30

LANES = 128
META_ROWS = 128
TQ = 256
TK = 256
FF_CHUNK = 256
HALO_ROWS = 8

VMEM_LIMIT = 56 * 1024 * 1024

_NT = (((1,), (1,)), ((), ()))


def _alibi_slope(h):
    start = 2.0 ** (-8.0 / N_DIFF_HEADS)
    return start ** (h + 1)


def _params(n_grid):
    return pltpu.CompilerParams(dimension_semantics=("arbitrary",) * n_grid,
                                vmem_limit_bytes=VMEM_LIMIT)


def _rms_scale(x):
    return lax.rsqrt(jnp.mean(x * x, axis=-1, keepdims=True) + EPS)


def _inproj_kernel(x_ref, g_ref, w_ref, o_ref):
    x = x_ref[...]
    xn = (x * _rms_scale(x) * g_ref[...]).astype(jnp.bfloat16)
    for c in range(PROJ_WIDTH // GROUP_WIDTH):
        cols = slice(c * GROUP_WIDTH, (c + 1) * GROUP_WIDTH)
        o_ref[:, cols] = jnp.dot(xn, w_ref[:, cols],
                                 preferred_element_type=jnp.float32).astype(jnp.bfloat16)


def _inproj(rows, g, w, tm):
    n = rows.shape[0]
    return pl.pallas_call(
        _inproj_kernel,
        grid=(n // tm,),
        in_specs=[pl.BlockSpec((tm, D_MODEL), lambda i: (i, 0)),
                  pl.BlockSpec((1, D_MODEL), lambda i: (0, 0)),
                  pl.BlockSpec((D_MODEL, PROJ_WIDTH), lambda i: (0, 0))],
        out_specs=pl.BlockSpec((tm, PROJ_WIDTH), lambda i: (i, 0)),
        out_shape=jax.ShapeDtypeStruct((n, PROJ_WIDTH), jnp.bfloat16),
        compiler_params=_params(1),
        name="inproj",
    )(rows, g, w)


def _lambda_value(lam_ref):
    lam = lam_ref[...]
    s1 = jnp.sum(lam[0:1] * lam[1:2], axis=-1, keepdims=True)
    s2 = jnp.sum(lam[2:3] * lam[3:4], axis=-1, keepdims=True)
    return jnp.exp(s1) - jnp.exp(s2) + LAMBDA_INIT


def _softplus(z):
    return jnp.maximum(z, 0.0) + jnp.log(1.0 + jnp.exp(-jnp.abs(z)))


def _lane_chunks(x):
    return [x[:, c:c + LANES] for c in range(0, x.shape[1], LANES)]


def _diff_update(idx, s, v_aug, dm_ref, dacc_ref):
    m_old = dm_ref[idx]
    m_new = jnp.maximum(m_old, jnp.max(s, axis=-1, keepdims=True))
    alpha = jnp.exp(m_old - m_new)
    p = jnp.concatenate([jnp.exp(sc - m_new) for sc in _lane_chunks(s)], axis=1)
    pv = jnp.dot(p.astype(jnp.bfloat16), v_aug, preferred_element_type=jnp.float32)
    for c in (0, LANES):
        dacc_ref[idx, :, c:c + LANES] = alpha * dacc_ref[idx, :, c:c + LANES] + pv[:, c:c + LANES]
    dm_ref[idx] = m_new


def _diff_block(qdm_ref, k_of, v_of, bias_of, mask, dm_ref, dacc_ref):
    for h in range(N_DIFF_HEADS):
        k, v = k_of(h), v_of(h)
        v_aug = jnp.concatenate([v, jnp.ones_like(v)], axis=1)
        bias = bias_of(h)
        for which in range(2):
            idx = 2 * h + which
            s = lax.dot_general(qdm_ref[idx], k, _NT, preferred_element_type=jnp.float32) + bias
            if mask is not None:
                s = jnp.where(mask, s, NEG)
            _diff_update(idx, s, v_aug, dm_ref, dacc_ref)


def _sb_block(qsm_ref, k_of, v_of, tri, mask, lo_lanes, sc_ref, sacc_ref):
    tri2 = jnp.concatenate([tri, tri], axis=0)
    for pair in range(N_SB_HEADS // 2):
        k, v = k_of(pair), v_of(pair)
        zero = jnp.zeros_like(v)
        v_split = jnp.concatenate([jnp.where(lo_lanes, v, zero), jnp.where(lo_lanes, zero, v)], axis=0)
        a_parts = []
        for half in range(2):
            idx = 2 * pair + half
            z = lax.dot_general(qsm_ref[idx], k, _NT, preferred_element_type=jnp.float32)
            sp = _softplus(z)
            if mask is not None:
                sp = jnp.where(mask, sp, 0.0)
            hi = sp.astype(jnp.bfloat16)
            lo = (sp - hi.astype(jnp.float32)).astype(jnp.bfloat16)
            cum = jnp.dot(jnp.concatenate([hi, lo], axis=1), tri2, preferred_element_type=jnp.float32)
            c_old = sc_ref[idx]
            a = jnp.concatenate([jnp.exp(zc - cc - c_old)
                                 for zc, cc in zip(_lane_chunks(z), _lane_chunks(cum))], axis=1)
            if mask is not None:
                a = jnp.where(mask, a, 0.0)
            a_parts.append(a.astype(jnp.bfloat16))
            sc_ref[idx] = c_old + jnp.sum(sp, axis=-1, keepdims=True)
        sacc_ref[pair] += jnp.dot(jnp.concatenate(a_parts, axis=1), v_split,
                                  preferred_element_type=jnp.float32)


def _attn_body(lam_ref, gd_ref, gs_ref, qd_ref, qs_ref, kd_ref, vd_ref, ks_ref, vs_ref,
               kdm_ref, vdm_ref, ksm_ref, vsm_ref, o_ref,
               qdm_ref, qsm_ref, dm_ref, dacc_ref, sc_ref, sacc_ref, tri_ref, *, tq, meta_only):
    j = 0 if meta_only else pl.program_id(1)
    lam = _lambda_value(lam_ref)
    lane = lax.broadcasted_iota(jnp.int32, (1, LANES), 1)
    lo_lanes = lane < DIFF_QK_DIM
    tri_r = lax.broadcasted_iota(jnp.int32, (TK, TK), 0)
    tri_c = lax.broadcasted_iota(jnp.int32, (TK, TK), 1)
    tri_ref[...] = jnp.where(tri_r >= tri_c, 1.0, 0.0).astype(jnp.bfloat16)

    for src, dst in ((qd_ref, qdm_ref), (qs_ref, qsm_ref)):
        for h in range(4):
            q = src[0, :, h * LANES:(h + 1) * LANES]
            zero = jnp.zeros_like(q)
            dst[2 * h] = jnp.where(lo_lanes, q, zero)
            dst[2 * h + 1] = jnp.where(lo_lanes, zero, q)
    dm_ref[...] = jnp.full(dm_ref.shape, NEG, jnp.float32)
    dacc_ref[...] = jnp.zeros(dacc_ref.shape, jnp.float32)
    sc_ref[...] = jnp.zeros(sc_ref.shape, jnp.float32)
    sacc_ref[...] = jnp.zeros(sacc_ref.shape, jnp.float32)

    q0 = 0 if meta_only else N_META + j * tq
    meta_pos = lax.broadcasted_iota(jnp.int32, (1, META_ROWS), 1)
    blk_pos = lax.broadcasted_iota(jnp.int32, (1, TK), 1)
    mrow = lax.broadcasted_iota(jnp.int32, (tq, META_ROWS), 0)
    mcol = lax.broadcasted_iota(jnp.int32, (tq, META_ROWS), 1)
    meta_valid = mcol < N_META
    head_cols = lambda h: slice(h * LANES, (h + 1) * LANES)

    def real_block(ref, start):
        return lambda h: ref[0, pl.ds(start, TK), head_cols(h)]

    def meta_block(ref):
        return lambda h: ref[:, head_cols(h)]

    def diff_meta(mask):
        _diff_block(qdm_ref, meta_block(kdm_ref), meta_block(vdm_ref),
                    lambda h: _alibi_slope(h) * (meta_pos - q0).astype(jnp.float32),
                    mask, dm_ref, dacc_ref)

    def sb_meta(mask):
        _sb_block(qsm_ref, meta_block(ksm_ref), meta_block(vsm_ref),
                  tri_ref[0:META_ROWS, 0:META_ROWS], mask, lo_lanes, sc_ref, sacc_ref)

    if meta_only:
        diff_meta(meta_valid & (mcol <= mrow))
        sb_meta(meta_valid & (mcol < mrow))
    else:
        row = lax.broadcasted_iota(jnp.int32, (tq, TK), 0)
        col = lax.broadcasted_iota(jnp.int32, (tq, TK), 1)
        diag = pl.multiple_of(j * TK, TK)
        diff_meta(meta_valid)
        _sb_block(qsm_ref, real_block(ks_ref, diag), real_block(vs_ref, diag),
                  tri_ref[...], col < row, lo_lanes, sc_ref, sacc_ref)

        def mid(t, carry):
            d_start = pl.multiple_of(t * TK, TK)
            s_start = pl.multiple_of((j - 1 - t) * TK, TK)
            _diff_block(qdm_ref, real_block(kd_ref, d_start), real_block(vd_ref, d_start),
                        lambda h: _alibi_slope(h) * ((t - j) * TK + blk_pos).astype(jnp.float32),
                        None, dm_ref, dacc_ref)
            _sb_block(qsm_ref, real_block(ks_ref, s_start), real_block(vs_ref, s_start),
                      tri_ref[...], None, lo_lanes, sc_ref, sacc_ref)
            return carry
        lax.fori_loop(0, j, mid, 0)

        _diff_block(qdm_ref, real_block(kd_ref, diag), real_block(vd_ref, diag),
                    lambda h: _alibi_slope(h) * blk_pos.astype(jnp.float32),
                    col <= row, dm_ref, dacc_ref)
        sb_meta(meta_valid)

    for h in range(N_DIFF_HEADS):
        o = (dacc_ref[2 * h, :, 0:LANES] / dacc_ref[2 * h, :, LANES:2 * LANES]
             - lam * (dacc_ref[2 * h + 1, :, 0:LANES] / dacc_ref[2 * h + 1, :, LANES:2 * LANES]))
        o = o * _rms_scale(o) * gd_ref[...] * (1.0 - LAMBDA_INIT)
        o_ref[0, :, head_cols(h)] = o.astype(o_ref.dtype)
    for pair in range(N_SB_HEADS // 2):
        o = sacc_ref[pair]
        sq = o * o
        ms_lo = jnp.sum(jnp.where(lo_lanes, sq, 0.0), axis=-1, keepdims=True)
        ms_hi = jnp.sum(jnp.where(lo_lanes, 0.0, sq), axis=-1, keepdims=True)
        ms = jnp.where(lo_lanes, ms_lo, ms_hi) * (1.0 / SB_HEAD_DIM)
        o = o * lax.rsqrt(ms + EPS) * gs_ref[...]
        o_ref[0, :, GROUP_WIDTH + pair * LANES:GROUP_WIDTH + (pair + 1) * LANES] = o.astype(o_ref.dtype)


def _attn_scratch(tq):
    maps = 2 * N_DIFF_HEADS
    return [pltpu.VMEM((maps, tq, LANES), jnp.bfloat16),
            pltpu.VMEM((N_SB_HEADS, tq, LANES), jnp.bfloat16),
            pltpu.VMEM((maps, tq, LANES), jnp.float32),
            pltpu.VMEM((maps, tq, 2 * LANES), jnp.float32),
            pltpu.VMEM((N_SB_HEADS, tq, LANES), jnp.float32),
            pltpu.VMEM((N_SB_HEADS // 2, tq, LANES), jnp.float32),
            pltpu.VMEM((TK, TK), jnp.bfloat16)]


def _meta_kv_specs(index_map_for):
    return [pl.BlockSpec((META_ROWS, GROUP_WIDTH), index_map_for(g)) for g in (1, 2, 4, 5)]


def _attn(proj, proj_meta, lam, gd, gs, batch, seq):
    proj3 = proj.reshape(batch, seq, PROJ_WIDTH)
    small = [pl.BlockSpec(lam.shape, lambda b, j: (0, 0)),
             pl.BlockSpec(gd.shape, lambda b, j: (0, 0)),
             pl.BlockSpec(gs.shape, lambda b, j: (0, 0))]
    qspec = lambda g: pl.BlockSpec((1, TQ, GROUP_WIDTH), lambda b, j, g=g: (b, j, g))
    kvspec = lambda g: pl.BlockSpec((1, seq, GROUP_WIDTH), lambda b, j, g=g: (b, 0, g))
    meta_specs = _meta_kv_specs(lambda g: (lambda b, j, g=g: (0, g)))
    out = pl.pallas_call(
        functools.partial(_attn_body, tq=TQ, meta_only=False),
        grid=(batch, seq // TQ),
        in_specs=small + [qspec(0), qspec(3), kvspec(1), kvspec(2), kvspec(4), kvspec(5)] + meta_specs,
        out_specs=pl.BlockSpec((1, TQ, D_MODEL), lambda b, j: (b, j, 0)),
        out_shape=jax.ShapeDtypeStruct((batch, seq, D_MODEL), jnp.bfloat16),
        scratch_shapes=_attn_scratch(TQ),
        compiler_params=_params(2),
        name="attn",
    )(lam, gd, gs, proj3, proj3, proj3, proj3, proj3, proj3,
      proj_meta, proj_meta, proj_meta, proj_meta)
    return out.reshape(batch * seq, D_MODEL)


def _attn_meta_body(lam_ref, gd_ref, gs_ref, qd_ref, qs_ref, kdm_ref, vdm_ref, ksm_ref, vsm_ref,
                    o_ref, *scratch):
    _attn_body(lam_ref, gd_ref, gs_ref, qd_ref, qs_ref, None, None, None, None,
               kdm_ref, vdm_ref, ksm_ref, vsm_ref, o_ref, *scratch,
               tq=META_ROWS, meta_only=True)


def _attn_meta(proj_meta, lam, gd, gs):
    proj3 = proj_meta.reshape(1, META_ROWS, PROJ_WIDTH)
    small = [pl.BlockSpec(lam.shape, lambda i: (0, 0)),
             pl.BlockSpec(gd.shape, lambda i: (0, 0)),
             pl.BlockSpec(gs.shape, lambda i: (0, 0))]
    qspec = lambda g: pl.BlockSpec((1, META_ROWS, GROUP_WIDTH), lambda i, g=g: (0, 0, g))
    meta_specs = _meta_kv_specs(lambda g: (lambda i, g=g: (0, g)))
    out = pl.pallas_call(
        _attn_meta_body,
        grid=(1,),
        in_specs=small + [qspec(0), qspec(3)] + meta_specs,
        out_specs=pl.BlockSpec((1, META_ROWS, D_MODEL), lambda i: (0, 0, 0)),
        out_shape=jax.ShapeDtypeStruct((1, META_ROWS, D_MODEL), jnp.bfloat16),
        scratch_shapes=_attn_scratch(META_ROWS),
        compiler_params=_params(1),
        name="attn_meta",
    )(lam, gd, gs, proj3, proj3, proj_meta, proj_meta, proj_meta, proj_meta)
    return out.reshape(META_ROWS, D_MODEL)


def _outproj_kernel(x_ref, a_ref, w_ref, o_ref):
    o_ref[...] = x_ref[...] + jnp.dot(a_ref[...], w_ref[...], preferred_element_type=jnp.float32)


def _outproj(x_rows, merged, w, tm):
    n = x_rows.shape[0]
    return pl.pallas_call(
        _outproj_kernel,
        grid=(n // tm,),
        in_specs=[pl.BlockSpec((tm, D_MODEL), lambda i: (i, 0)),
                  pl.BlockSpec((tm, D_MODEL), lambda i: (i, 0)),
                  pl.BlockSpec((D_MODEL, D_MODEL), lambda i: (0, 0))],
        out_specs=pl.BlockSpec((tm, D_MODEL), lambda i: (i, 0)),
        out_shape=jax.ShapeDtypeStruct((n, D_MODEL), jnp.float32),
        compiler_params=_params(1),
        name="outproj",
    )(x_rows, merged, w)


def _meta_up_kernel(h_ref, g_ref, w_ref, o_ref):
    h = h_ref[...]
    xn = (h * _rms_scale(h) * g_ref[...]).astype(jnp.bfloat16)
    for c in range(2 * D_FF // FF_CHUNK):
        cols = slice(c * FF_CHUNK, (c + 1) * FF_CHUNK)
        o_ref[:, cols] = jnp.dot(xn, w_ref[:, cols], preferred_element_type=jnp.float32)


def _meta_up(h_meta, g, w):
    return pl.pallas_call(
        _meta_up_kernel,
        grid=(1,),
        in_specs=[pl.BlockSpec((META_ROWS, D_MODEL), lambda i: (0, 0)),
                  pl.BlockSpec((1, D_MODEL), lambda i: (0, 0)),
                  pl.BlockSpec((D_MODEL, 2 * D_FF), lambda i: (0, 0))],
        out_specs=pl.BlockSpec((META_ROWS, 2 * D_FF), lambda i: (0, 0)),
        out_shape=jax.ShapeDtypeStruct((META_ROWS, 2 * D_FF), jnp.float32),
        compiler_params=_params(1),
        name="meta_up",
    )(h_meta, g, w)


def _ffn_up_kernel(h_ref, g_ref, w_ref, cw_ref, cb_ref, halo0_ref, o_ref, halo_ref, *, tiles_per_seq):
    i = pl.program_id(0)

    @pl.when(i % tiles_per_seq == 0)
    def _():
        halo_ref[...] = halo0_ref[...]

    tm = h_ref.shape[0]
    h = h_ref[...]
    xn = (h * _rms_scale(h) * g_ref[...]).astype(jnp.bfloat16)
    row = lax.broadcasted_iota(jnp.int32, (tm, FF_CHUNK), 0)

    def conv(cols):
        u = jnp.dot(xn, w_ref[:, cols], preferred_element_type=jnp.float32)
        prev2 = halo_ref[HALO_ROWS - 2:HALO_ROWS - 1, cols]
        prev1 = halo_ref[HALO_ROWS - 1:HALO_ROWS, cols]
        u1 = jnp.where(row == 0, prev1, pltpu.roll(u, 1, 0))
        u2 = jnp.where(row == 0, prev2, jnp.where(row == 1, prev1, pltpu.roll(u, 2, 0)))
        halo_ref[:, cols] = u[tm - HALO_ROWS:, :]
        return cb_ref[:, cols] + cw_ref[0:1, cols] * u2 + cw_ref[1:2, cols] * u1 + cw_ref[2:3, cols] * u

    for c in range(D_FF // FF_CHUNK):
        gate = conv(slice(c * FF_CHUNK, (c + 1) * FF_CHUNK))
        up = conv(slice(D_FF + c * FF_CHUNK, D_FF + (c + 1) * FF_CHUNK))
        act = gate * (1.0 / (1.0 + jnp.exp(-gate))) * up
        o_ref[:, c * FF_CHUNK:(c + 1) * FF_CHUNK] = act.astype(o_ref.dtype)


def _ffn_up(h1, g, w, cw, cb, halo0, tm, seq):
    n = h1.shape[0]
    return pl.pallas_call(
        functools.partial(_ffn_up_kernel, tiles_per_seq=seq // tm),
        grid=(n // tm,),
        in_specs=[pl.BlockSpec((tm, D_MODEL), lambda i: (i, 0)),
                  pl.BlockSpec((1, D_MODEL), lambda i: (0, 0)),
                  pl.BlockSpec((D_MODEL, 2 * D_FF), lambda i: (0, 0)),
                  pl.BlockSpec((3, 2 * D_FF), lambda i: (0, 0)),
                  pl.BlockSpec((1, 2 * D_FF), lambda i: (0, 0)),
                  pl.BlockSpec((HALO_ROWS, 2 * D_FF), lambda i: (0, 0))],
        out_specs=pl.BlockSpec((tm, D_FF), lambda i: (i, 0)),
        out_shape=jax.ShapeDtypeStruct((n, D_FF), jnp.bfloat16),
        scratch_shapes=[pltpu.VMEM((HALO_ROWS, 2 * D_FF), jnp.float32)],
        compiler_params=_params(1),
        name="ffn_up",
    )(h1, g, w, cw, cb, halo0)


def _ffn_down_kernel(h_ref, a_ref, w_ref, g_ref, o_ref):
    y = h_ref[...] + jnp.dot(a_ref[...], w_ref[...], preferred_element_type=jnp.float32)
    o_ref[...] = y * _rms_scale(y) * g_ref[...]


def _ffn_down(h1, act, w, g, tm):
    n = h1.shape[0]
    return pl.pallas_call(
        _ffn_down_kernel,
        grid=(n // tm,),
        in_specs=[pl.BlockSpec((tm, D_MODEL), lambda i: (i, 0)),
                  pl.BlockSpec((tm, D_FF), lambda i: (i, 0)),
                  pl.BlockSpec((D_FF, D_MODEL), lambda i: (0, 0)),
                  pl.BlockSpec((1, D_MODEL), lambda i: (0, 0))],
        out_specs=pl.BlockSpec((tm, D_MODEL), lambda i: (i, 0)),
        out_shape=jax.ShapeDtypeStruct((n, D_MODEL), jnp.float32),
        compiler_params=_params(1),
        name="ffn_down",
    )(h1, act, w, g)


def kernel(x, meta_tokens, g_attn, w_in, lam_q1, lam_k1, lam_q2, lam_k2, g_diff, g_sb, w_out, g_ffn,
           w_up, conv_w, conv_b, w_down, g_final):
    batch, seq, _ = x.shape
    bf16 = jnp.bfloat16
    qscale = jnp.concatenate([jnp.full((GROUP_WIDTH,), DIFF_QK_DIM ** -0.5, jnp.float32),
                              jnp.ones((2 * GROUP_WIDTH,), jnp.float32),
                              jnp.full((GROUP_WIDTH,), SB_HEAD_DIM ** -0.5, jnp.float32),
                              jnp.ones((2 * GROUP_WIDTH,), jnp.float32)])
    w_in_b = (w_in[0] * qscale).astype(bf16)
    w_out_b = w_out[0].astype(bf16)
    w_up_b = w_up[0].astype(bf16)
    w_down_b = w_down[0].astype(bf16)
    lam = jnp.stack([lam_q1[0], lam_k1[0], lam_q2[0], lam_k2[0]]).astype(jnp.float32)
    gd = g_diff[0].reshape(1, DIFF_V_DIM)
    gs = jnp.tile(g_sb[0], 2).reshape(1, LANES)
    g_attn2, g_ffn2, g_final2 = g_attn[0].reshape(1, -1), g_ffn[0].reshape(1, -1), g_final.reshape(1, -1)

    x_rows = x.reshape(batch * seq, D_MODEL)
    meta_rows = jnp.pad(meta_tokens.astype(x.dtype), ((0, META_ROWS - N_META), (0, 0)))

    proj_meta = _inproj(meta_rows, g_attn2, w_in_b, META_ROWS)
    merged_meta = _attn_meta(proj_meta, lam, gd, gs)
    h1_meta = _outproj(meta_rows, merged_meta, w_out_b, META_ROWS)
    u_meta = _meta_up(h1_meta, g_ffn2, w_up_b)
    halo0 = jnp.pad(u_meta[N_META - 2:N_META], ((HALO_ROWS - 2, 0), (0, 0)))

    proj = _inproj(x_rows, g_attn2, w_in_b, 512)
    merged = _attn(proj, proj_meta, lam, gd, gs, batch, seq)
    h1 = _outproj(x_rows, merged, w_out_b, 512)
    act = _ffn_up(h1, g_ffn2, w_up_b, conv_w[0], conv_b[0].reshape(1, -1), halo0, 256, seq)
    out = _ffn_down(h1, act, w_down_b, g_final2, 512)
    return out.reshape(batch, seq, D_MODEL)
```

```python
import functools
import math

import jax
import jax.numpy as jnp
from jax import lax
from jax.experimental import pallas as pl
from jax.experimental.pallas import tpu as pltpu

D_MODEL = 1024
N_META = 16
N_DIFF_HEADS = 4
DIFF_QK_DIM = 64
DIFF_V_DIM = 128
N_SB_HEADS = 8
SB_HEAD_DIM = 64
GROUP_WIDTH = 512
PROJ_WIDTH = 6 * GROUP_WIDTH
D_FF = 2816
EPS = 1e-6
LAMBDA_INIT = 0.8 - 0.6 * math.exp(-0.3 * 0)
NEG = -1e30
LOG2E = math.log2(math.e)
EXP2_CLAMP = 126.0

LANES = 128
META_ROWS = 128
TQ = 256
TK = 256
FF_CHUNK = 256
FF_DOWN_GROUP = 4
HALO_ROWS = 8

VMEM_LIMIT = 56 * 1024 * 1024

_NT = (((1,), (1,)), ((), ()))


def _alibi_slope(h):
    start = 2.0 ** (-8.0 / N_DIFF_HEADS)
    return start ** (h + 1)


def _params(n_grid):
    return pltpu.CompilerParams(dimension_semantics=("arbitrary",) * n_grid,
                                vmem_limit_bytes=VMEM_LIMIT)


def _rms_scale(x):
    return lax.rsqrt(jnp.mean(x * x, axis=-1, keepdims=True) + EPS)


def _inproj_kernel(x_ref, g_ref, w_ref, o_ref):
    x = x_ref[...]
    xn = (x * _rms_scale(x) * g_ref[...]).astype(jnp.bfloat16)
    for c in range(PROJ_WIDTH // GROUP_WIDTH):
        cols = slice(c * GROUP_WIDTH, (c + 1) * GROUP_WIDTH)
        o_ref[:, cols] = jnp.dot(xn, w_ref[:, cols],
                                 preferred_element_type=jnp.float32).astype(jnp.bfloat16)


def _inproj(rows, g, w, tm):
    n = rows.shape[0]
    return pl.pallas_call(
        _inproj_kernel,
        grid=(n // tm,),
        in_specs=[pl.BlockSpec((tm, D_MODEL), lambda i: (i, 0)),
                  pl.BlockSpec((1, D_MODEL), lambda i: (0, 0)),
                  pl.BlockSpec((D_MODEL, PROJ_WIDTH), lambda i: (0, 0))],
        out_specs=pl.BlockSpec((tm, PROJ_WIDTH), lambda i: (i, 0)),
        out_shape=jax.ShapeDtypeStruct((n, PROJ_WIDTH), jnp.bfloat16),
        compiler_params=_params(1),
        name="inproj",
    )(rows, g, w)


def _lambda_value(lam_ref):
    lam = lam_ref[...]
    s1 = jnp.sum(lam[0:1] * lam[1:2], axis=-1, keepdims=True)
    s2 = jnp.sum(lam[2:3] * lam[3:4], axis=-1, keepdims=True)
    return jnp.exp(s1) - jnp.exp(s2) + LAMBDA_INIT


def _softplus_log2(z):
    return jnp.maximum(z, jnp.log2(1.0 + jnp.exp2(jnp.minimum(z, EXP2_CLAMP))))


def _lane_chunks(x):
    return [x[:, c:c + LANES] for c in range(0, x.shape[1], LANES)]


def _diff_update(idx, rows, s, v_aug, dm_ref, dacc_ref):
    m_old = dm_ref[idx, rows]
    m_new = jnp.maximum(m_old, jnp.max(s, axis=-1, keepdims=True))
    alpha = jnp.exp2(m_old - m_new)
    p = jnp.concatenate([jnp.exp2(sc - m_new) for sc in _lane_chunks(s)], axis=1)
    pv = jnp.dot(p.astype(jnp.bfloat16), v_aug, preferred_element_type=jnp.float32)
    for c in (0, LANES):
        dacc_ref[idx, rows, c:c + LANES] = alpha * dacc_ref[idx, rows, c:c + LANES] + pv[:, c:c + LANES]
    dm_ref[idx, rows] = m_new


def _diff_block(qdm_ref, k_of, v_of, bias_of, mask, dm_ref, dacc_ref, rows=slice(None)):
    for h in range(N_DIFF_HEADS):
        k, v = k_of(h), v_of(h)
        v_aug = jnp.concatenate([v, jnp.ones_like(v)], axis=1)
        bias = bias_of(h)
        for which in range(2):
            idx = 2 * h + which
            s = lax.dot_general(qdm_ref[idx, rows], k, _NT, preferred_element_type=jnp.float32) + bias
            if mask is not None:
                s = jnp.where(mask, s, NEG)
            _diff_update(idx, rows, s, v_aug, dm_ref, dacc_ref)


def _sb_block(qsm_ref, k_of, v_of, tri, mask, lo_lanes, sc_ref, sacc_ref, rows=slice(None)):
    tri2 = jnp.concatenate([tri, tri], axis=0)
    for pair in range(N_SB_HEADS // 2):
        k, v = k_of(pair), v_of(pair)
        zero = jnp.zeros_like(v)
        v_split = jnp.concatenate([jnp.where(lo_lanes, v, zero), jnp.where(lo_lanes, zero, v)], axis=0)
        a_parts = []
        for half in range(2):
            idx = 2 * pair + half
            z = lax.dot_general(qsm_ref[idx, rows], k, _NT, preferred_element_type=jnp.float32)
            sp = _softplus_log2(z)
            if mask is not None:
                sp = jnp.where(mask, sp, 0.0)
            hi = sp.astype(jnp.bfloat16)
            lo = (sp - hi.astype(jnp.float32)).astype(jnp.bfloat16)
            cum = jnp.dot(jnp.concatenate([hi, lo], axis=1), tri2, preferred_element_type=jnp.float32)
            c_old = sc_ref[idx, rows]
            a = jnp.concatenate([jnp.exp2(zc - cc - c_old)
                                 for zc, cc in zip(_lane_chunks(z), _lane_chunks(cum))], axis=1)
            if mask is not None:
                a = jnp.where(mask, a, 0.0)
            a_parts.append(a.astype(jnp.bfloat16))
            sc_ref[idx, rows] = c_old + jnp.sum(sp, axis=-1, keepdims=True)
        sacc_ref[pair, rows] += jnp.dot(jnp.concatenate(a_parts, axis=1), v_split,
                                        preferred_element_type=jnp.float32)


def _attn_body(lam_ref, gd_ref, gs_ref, qd_ref, qs_ref, kd_ref, vd_ref, ks_ref, vs_ref,
               kdm_ref, vdm_ref, ksm_ref, vsm_ref, o_ref,
               qdm_ref, qsm_ref, dm_ref, dacc_ref, sc_ref, sacc_ref, tri_ref, *, tq, meta_only):
    j = 0 if meta_only else pl.program_id(1)
    lam = _lambda_value(lam_ref)
    lane = lax.broadcasted_iota(jnp.int32, (1, LANES), 1)
    lo_lanes = lane < DIFF_QK_DIM
    tri_r = lax.broadcasted_iota(jnp.int32, (TK, TK), 0)
    tri_c = lax.broadcasted_iota(jnp.int32, (TK, TK), 1)
    tri_ref[...] = jnp.where(tri_r >= tri_c, 1.0, 0.0).astype(jnp.bfloat16)

    for src, dst in ((qd_ref, qdm_ref), (qs_ref, qsm_ref)):
        for h in range(4):
            q = src[0, :, h * LANES:(h + 1) * LANES]
            zero = jnp.zeros_like(q)
            dst[2 * h] = jnp.where(lo_lanes, q, zero)
            dst[2 * h + 1] = jnp.where(lo_lanes, zero, q)
    dm_ref[...] = jnp.full(dm_ref.shape, NEG, jnp.float32)
    dacc_ref[...] = jnp.zeros(dacc_ref.shape, jnp.float32)
    sc_ref[...] = jnp.zeros(sc_ref.shape, jnp.float32)
    sacc_ref[...] = jnp.zeros(sacc_ref.shape, jnp.float32)

    q0 = 0 if meta_only else N_META + j * tq
    meta_pos = lax.broadcasted_iota(jnp.int32, (1, META_ROWS), 1)
    blk_pos = lax.broadcasted_iota(jnp.int32, (1, TK), 1)
    mrow = lax.broadcasted_iota(jnp.int32, (tq, META_ROWS), 0)
    mcol = lax.broadcasted_iota(jnp.int32, (tq, META_ROWS), 1)
    meta_valid = mcol < N_META
    head_cols = lambda h: slice(h * LANES, (h + 1) * LANES)

    def real_block(ref, start, size=TK):
        return lambda h: ref[0, pl.ds(start, size), head_cols(h)]

    def meta_block(ref):
        return lambda h: ref[:, head_cols(h)]

    def diff_meta(mask):
        _diff_block(qdm_ref, meta_block(kdm_ref), meta_block(vdm_ref),
                    lambda h: _alibi_slope(h) * LOG2E * (meta_pos - q0).astype(jnp.float32),
                    mask, dm_ref, dacc_ref)

    def sb_meta(mask):
        _sb_block(qsm_ref, meta_block(ksm_ref), meta_block(vsm_ref),
                  tri_ref[0:META_ROWS, 0:META_ROWS], mask, lo_lanes, sc_ref, sacc_ref)

    if meta_only:
        diff_meta(meta_valid & (mcol <= mrow))
        sb_meta(meta_valid & (mcol < mrow))
    else:
        row = lax.broadcasted_iota(jnp.int32, (tq, TK), 0)
        col = lax.broadcasted_iota(jnp.int32, (tq, TK), 1)
        diag = pl.multiple_of(j * TK, TK)
        diff_meta(meta_valid)
        _sb_block(qsm_ref, real_block(ks_ref, diag), real_block(vs_ref, diag),
                  tri_ref[...], col < row, lo_lanes, sc_ref, sacc_ref)

        def mid(t, carry):
            d_start = pl.multiple_of(t * TK, TK)
            s_start = pl.multiple_of((j - 1 - t) * TK, TK)
            _diff_block(qdm_ref, real_block(kd_ref, d_start), real_block(vd_ref, d_start),
                        lambda h: _alibi_slope(h) * LOG2E * ((t - j) * TK + blk_pos).astype(jnp.float32),
                        None, dm_ref, dacc_ref)
            _sb_block(qsm_ref, real_block(ks_ref, s_start), real_block(vs_ref, s_start),
                      tri_ref[...], None, lo_lanes, sc_ref, sacc_ref)
            return carry
        lax.fori_loop(0, j, mid, 0)

        _diff_block(qdm_ref, real_block(kd_ref, diag), real_block(vd_ref, diag),
                    lambda h: _alibi_slope(h) * LOG2E * blk_pos.astype(jnp.float32),
                    col <= row, dm_ref, dacc_ref)
        sb_meta(meta_valid)

    for h in range(N_DIFF_HEADS):
        o = (dacc_ref[2 * h, :, 0:LANES] / dacc_ref[2 * h, :, LANES:2 * LANES]
             - lam * (dacc_ref[2 * h + 1, :, 0:LANES] / dacc_ref[2 * h + 1, :, LANES:2 * LANES]))
        o = o * _rms_scale(o) * gd_ref[...] * (1.0 - LAMBDA_INIT)
        o_ref[0, :, head_cols(h)] = o.astype(o_ref.dtype)
    for pair in range(N_SB_HEADS // 2):
        o = sacc_ref[pair]
        sq = o * o
        ms_lo = jnp.sum(jnp.where(lo_lanes, sq, 0.0), axis=-1, keepdims=True)
        ms_hi = jnp.sum(jnp.where(lo_lanes, 0.0, sq), axis=-1, keepdims=True)
        ms = jnp.where(lo_lanes, ms_lo, ms_hi) * (1.0 / SB_HEAD_DIM)
        o = o * lax.rsqrt(ms + EPS) * gs_ref[...]
        o_ref[0, :, GROUP_WIDTH + pair * LANES:GROUP_WIDTH + (pair + 1) * LANES] = o.astype(o_ref.dtype)


def _attn_scratch(tq):
    maps = 2 * N_DIFF_HEADS
    return [pltpu.VMEM((maps, tq, LANES), jnp.bfloat16),
            pltpu.VMEM((N_SB_HEADS, tq, LANES), jnp.bfloat16),
            pltpu.VMEM((maps, tq, LANES), jnp.float32),
            pltpu.VMEM((maps, tq, 2 * LANES), jnp.float32),
            pltpu.VMEM((N_SB_HEADS, tq, LANES), jnp.float32),
            pltpu.VMEM((N_SB_HEADS // 2, tq, LANES), jnp.float32),
            pltpu.VMEM((TK, TK), jnp.bfloat16)]


def _meta_kv_specs(index_map_for):
    return [pl.BlockSpec((META_ROWS, GROUP_WIDTH), index_map_for(g)) for g in (1, 2, 4, 5)]


def _attn(proj, proj_meta, lam, gd, gs, batch, seq):
    proj3 = proj.reshape(batch, seq, PROJ_WIDTH)
    small = [pl.BlockSpec(lam.shape, lambda b, j: (0, 0)),
             pl.BlockSpec(gd.shape, lambda b, j: (0, 0)),
             pl.BlockSpec(gs.shape, lambda b, j: (0, 0))]
    qspec = lambda g: pl.BlockSpec((1, TQ, GROUP_WIDTH), lambda b, j, g=g: (b, j, g))
    kvspec = lambda g: pl.BlockSpec((1, seq, GROUP_WIDTH), lambda b, j, g=g: (b, 0, g))
    meta_specs = _meta_kv_specs(lambda g: (lambda b, j, g=g: (0, g)))
    out = pl.pallas_call(
        functools.partial(_attn_body, tq=TQ, meta_only=False),
        grid=(batch, seq // TQ),
        in_specs=small + [qspec(0), qspec(3), kvspec(1), kvspec(2), kvspec(4), kvspec(5)] + meta_specs,
        out_specs=pl.BlockSpec((1, TQ, D_MODEL), lambda b, j: (b, j, 0)),
        out_shape=jax.ShapeDtypeStruct((batch, seq, D_MODEL), jnp.bfloat16),
        scratch_shapes=_attn_scratch(TQ),
        compiler_params=_params(2),
        name="attn",
    )(lam, gd, gs, proj3, proj3, proj3, proj3, proj3, proj3,
      proj_meta, proj_meta, proj_meta, proj_meta)
    return out.reshape(batch * seq, D_MODEL)


def _attn_meta_body(lam_ref, gd_ref, gs_ref, qd_ref, qs_ref, kdm_ref, vdm_ref, ksm_ref, vsm_ref,
                    o_ref, *scratch):
    _attn_body(lam_ref, gd_ref, gs_ref, qd_ref, qs_ref, None, None, None, None,
               kdm_ref, vdm_ref, ksm_ref, vsm_ref, o_ref, *scratch,
               tq=META_ROWS, meta_only=True)


def _attn_meta(proj_meta, lam, gd, gs):
    proj3 = proj_meta.reshape(1, META_ROWS, PROJ_WIDTH)
    small = [pl.BlockSpec(lam.shape, lambda i: (0, 0)),
             pl.BlockSpec(gd.shape, lambda i: (0, 0)),
             pl.BlockSpec(gs.shape, lambda i: (0, 0))]
    qspec = lambda g: pl.BlockSpec((1, META_ROWS, GROUP_WIDTH), lambda i, g=g: (0, 0, g))
    meta_specs = _meta_kv_specs(lambda g: (lambda i, g=g: (0, g)))
    out = pl.pallas_call(
        _attn_meta_body,
        grid=(1,),
        in_specs=small + [qspec(0), qspec(3)] + meta_specs,
        out_specs=pl.BlockSpec((1, META_ROWS, D_MODEL), lambda i: (0, 0, 0)),
        out_shape=jax.ShapeDtypeStruct((1, META_ROWS, D_MODEL), jnp.bfloat16),
        scratch_shapes=_attn_scratch(META_ROWS),
        compiler_params=_params(1),
        name="attn_meta",
    )(lam, gd, gs, proj3, proj3, proj_meta, proj_meta, proj_meta, proj_meta)
    return out.reshape(META_ROWS, D_MODEL)


def _outproj_kernel(x_ref, a_ref, w_ref, o_ref):
    o_ref[...] = x_ref[...] + jnp.dot(a_ref[...], w_ref[...], preferred_element_type=jnp.float32)


def _outproj(x_rows, merged, w, tm):
    n = x_rows.shape[0]
    return pl.pallas_call(
        _outproj_kernel,
        grid=(n // tm,),
        in_specs=[pl.BlockSpec((tm, D_MODEL), lambda i: (i, 0)),
                  pl.BlockSpec((tm, D_MODEL), lambda i: (i, 0)),
                  pl.BlockSpec((D_MODEL, D_MODEL), lambda i: (0, 0))],
        out_specs=pl.BlockSpec((tm, D_MODEL), lambda i: (i, 0)),
        out_shape=jax.ShapeDtypeStruct((n, D_MODEL), jnp.float32),
        compiler_params=_params(1),
        name="outproj",
    )(x_rows, merged, w)


def _meta_up_kernel(h_ref, g_ref, w_ref, o_ref):
    h = h_ref[...]
    xn = (h * _rms_scale(h) * g_ref[...]).astype(jnp.bfloat16)
    for c in range(2 * D_FF // FF_CHUNK):
        cols = slice(c * FF_CHUNK, (c + 1) * FF_CHUNK)
        o_ref[:, cols] = jnp.dot(xn, w_ref[:, cols], preferred_element_type=jnp.float32)


def _meta_up(h_meta, g, w):
    return pl.pallas_call(
        _meta_up_kernel,
        grid=(1,),
        in_specs=[pl.BlockSpec((META_ROWS, D_MODEL), lambda i: (0, 0)),
                  pl.BlockSpec((1, D_MODEL), lambda i: (0, 0)),
                  pl.BlockSpec((D_MODEL, 2 * D_FF), lambda i: (0, 0))],
        out_specs=pl.BlockSpec((META_ROWS, 2 * D_FF), lambda i: (0, 0)),
        out_shape=jax.ShapeDtypeStruct((META_ROWS, 2 * D_FF), jnp.float32),
        compiler_params=_params(1),
        name="meta_up",
    )(h_meta, g, w)


def _ffn_kernel(x_ref, a_ref, wo_ref, g_ref, wu_ref, cw_ref, cb_ref, halo0_ref, wd_ref, gf_ref, o_ref,
                halo_ref, act_ref, *, tiles_per_seq):
    i = pl.program_id(0)

    @pl.when(i % tiles_per_seq == 0)
    def _():
        halo_ref[...] = halo0_ref[...]

    tm = x_ref.shape[0]
    h1 = x_ref[...] + jnp.dot(a_ref[...], wo_ref[...], preferred_element_type=jnp.float32)
    xn = (h1 * _rms_scale(h1) * g_ref[...]).astype(jnp.bfloat16)
    row = lax.broadcasted_iota(jnp.int32, (tm, FF_CHUNK), 0)

    def conv(cols):
        u = jnp.dot(xn, wu_ref[:, cols], preferred_element_type=jnp.float32)
        prev2 = halo_ref[HALO_ROWS - 2:HALO_ROWS - 1, cols]
        prev1 = halo_ref[HALO_ROWS - 1:HALO_ROWS, cols]
        u1 = jnp.where(row == 0, prev1, pltpu.roll(u, 1, 0))
        u2 = jnp.where(row == 0, prev2, jnp.where(row == 1, prev1, pltpu.roll(u, 2, 0)))
        halo_ref[:, cols] = u[tm - HALO_ROWS:, :]
        return cb_ref[:, cols] + cw_ref[0:1, cols] * u2 + cw_ref[1:2, cols] * u1 + cw_ref[2:3, cols] * u

    y = h1
    n_chunks = D_FF // FF_CHUNK
    for c0 in range(0, n_chunks, FF_DOWN_GROUP):
        c1 = min(c0 + FF_DOWN_GROUP, n_chunks)
        for c in range(c0, c1):
            gate = conv(slice(c * FF_CHUNK, (c + 1) * FF_CHUNK))
            up = conv(slice(D_FF + c * FF_CHUNK, D_FF + (c + 1) * FF_CHUNK))
            act = gate * (1.0 / (1.0 + jnp.exp(-gate))) * up
            act_ref[:, c * FF_CHUNK:(c + 1) * FF_CHUNK] = act.astype(act_ref.dtype)
        ks = slice(c0 * FF_CHUNK, c1 * FF_CHUNK)
        y = y + jnp.dot(act_ref[:, ks], wd_ref[ks, :], preferred_element_type=jnp.float32)
    o_ref[...] = y * _rms_scale(y) * gf_ref[...]


def _ffn(x_rows, merged, wo, g, wu, cw, cb, halo0, wd, gf, tm, seq):
    n = x_rows.shape[0]
    const = lambda shape: pl.BlockSpec(shape, lambda i: (0, 0))
    return pl.pallas_call(
        functools.partial(_ffn_kernel, tiles_per_seq=seq // tm),
        grid=(n // tm,),
        in_specs=[pl.BlockSpec((tm, D_MODEL), lambda i: (i, 0)),
                  pl.BlockSpec((tm, D_MODEL), lambda i: (i, 0)),
                  const((D_MODEL, D_MODEL)),
                  const((1, D_MODEL)),
                  const((D_MODEL, 2 * D_FF)),
                  const((3, 2 * D_FF)),
                  const((1, 2 * D_FF)),
                  const((HALO_ROWS, 2 * D_FF)),
                  const((D_FF, D_MODEL)),
                  const((1, D_MODEL))],
        out_specs=pl.BlockSpec((tm, D_MODEL), lambda i: (i, 0)),
        out_shape=jax.ShapeDtypeStruct((n, D_MODEL), jnp.float32),
        scratch_shapes=[pltpu.VMEM((HALO_ROWS, 2 * D_FF), jnp.float32),
                        pltpu.VMEM((tm, D_FF), jnp.bfloat16)],
        compiler_params=_params(1),
        name="ffn",
    )(x_rows, merged, wo, g, wu, cw, cb, halo0, wd, gf)


def kernel(x, meta_tokens, g_attn, w_in, lam_q1, lam_k1, lam_q2, lam_k2, g_diff, g_sb, w_out, g_ffn,
           w_up, conv_w, conv_b, w_down, g_final):
    batch, seq, _ = x.shape
    bf16 = jnp.bfloat16
    qscale = jnp.concatenate([jnp.full((GROUP_WIDTH,), DIFF_QK_DIM ** -0.5 * LOG2E, jnp.float32),
                              jnp.ones((2 * GROUP_WIDTH,), jnp.float32),
                              jnp.full((GROUP_WIDTH,), SB_HEAD_DIM ** -0.5 * LOG2E, jnp.float32),
                              jnp.ones((2 * GROUP_WIDTH,), jnp.float32)])
    w_in_b = (w_in[0] * qscale).astype(bf16)
    w_out_b = w_out[0].astype(bf16)
    w_up_b = w_up[0].astype(bf16)
    w_down_b = w_down[0].astype(bf16)
    lam = jnp.stack([lam_q1[0], lam_k1[0], lam_q2[0], lam_k2[0]]).astype(jnp.float32)
    gd = g_diff[0].reshape(1, DIFF_V_DIM)
    gs = jnp.tile(g_sb[0], 2).reshape(1, LANES)
    g_attn2, g_ffn2, g_final2 = g_attn[0].reshape(1, -1), g_ffn[0].reshape(1, -1), g_final.reshape(1, -1)

    x_rows = x.reshape(batch * seq, D_MODEL)
    meta_rows = jnp.pad(meta_tokens.astype(x.dtype), ((0, META_ROWS - N_META), (0, 0)))

    proj_meta = _inproj(meta_rows, g_attn2, w_in_b, META_ROWS)
    merged_meta = _attn_meta(proj_meta, lam, gd, gs)
    h1_meta = _outproj(meta_rows, merged_meta, w_out_b, META_ROWS)
    u_meta = _meta_up(h1_meta, g_ffn2, w_up_b)
    halo0 = jnp.pad(u_meta[N_META - 2:N_META], ((HALO_ROWS - 2, 0), (0, 0)))

    proj = _inproj(x_rows, g_attn2, w_in_b, 512)
    merged = _attn(proj, proj_meta, lam, gd, gs, batch, seq)
    out = _ffn(x_rows, merged, w_out_b, g_ffn2, w_up_b, conv_w[0], conv_b[0].reshape(1, -1), halo0,
               w_down_b, g_final2, 256, seq)
    return out.reshape(batch, seq, D_MODEL)
```

```python
import functools
import math

import jax
import jax.numpy as jnp
from jax import lax
from jax.experimental import pallas as pl
from jax.experimental.pallas import tpu as pltpu

D_MODEL = 1024
N_META = 16
N_DIFF_HEADS = 4
DIFF_QK_DIM = 64
DIFF_V_DIM = 128
N_SB_HEADS = 8
SB_HEAD_DIM = 64
GROUP_WIDTH = 512
PROJ_WIDTH = 6 * GROUP_WIDTH
D_FF = 2816
EPS = 1e-6
LAMBDA_INIT = 0.8 - 0.6 * math.exp(-0.3 * 0)
NEG = -1e30
LOG2E = math.log2(math.e)
EXP2_CLAMP = 126.0

LANES = 128
META_ROWS = 128
TQ = 256
TK = 256
FF_CHUNK = 256
FF_DOWN_GROUP = 4
HALO_ROWS = 8

VMEM_LIMIT = 56 * 1024 * 1024

_NT = (((1,), (1,)), ((), ()))


def _alibi_slope(h):
    start = 2.0 ** (-8.0 / N_DIFF_HEADS)
    return start ** (h + 1)


def _params(n_grid):
    return pltpu.CompilerParams(dimension_semantics=("arbitrary",) * n_grid,
                                vmem_limit_bytes=VMEM_LIMIT)


def _rms_scale(x):
    return lax.rsqrt(jnp.mean(x * x, axis=-1, keepdims=True) + EPS)


def _inproj_kernel(x_ref, g_ref, w_ref, o_ref):
    x = x_ref[...]
    xn = (x * _rms_scale(x) * g_ref[...]).astype(jnp.bfloat16)
    for c in range(PROJ_WIDTH // GROUP_WIDTH):
        cols = slice(c * GROUP_WIDTH, (c + 1) * GROUP_WIDTH)
        o_ref[:, cols] = jnp.dot(xn, w_ref[:, cols],
                                 preferred_element_type=jnp.float32).astype(jnp.bfloat16)


def _inproj(rows, g, w, tm):
    n = rows.shape[0]
    return pl.pallas_call(
        _inproj_kernel,
        grid=(n // tm,),
        in_specs=[pl.BlockSpec((tm, D_MODEL), lambda i: (i, 0)),
                  pl.BlockSpec((1, D_MODEL), lambda i: (0, 0)),
                  pl.BlockSpec((D_MODEL, PROJ_WIDTH), lambda i: (0, 0))],
        out_specs=pl.BlockSpec((tm, PROJ_WIDTH), lambda i: (i, 0)),
        out_shape=jax.ShapeDtypeStruct((n, PROJ_WIDTH), jnp.bfloat16),
        compiler_params=_params(1),
        name="inproj",
    )(rows, g, w)


def _lambda_value(lam_ref):
    lam = lam_ref[...]
    s1 = jnp.sum(lam[0:1] * lam[1:2], axis=-1, keepdims=True)
    s2 = jnp.sum(lam[2:3] * lam[3:4], axis=-1, keepdims=True)
    return jnp.exp(s1) - jnp.exp(s2) + LAMBDA_INIT


def _softplus_log2(z):
    return jnp.maximum(z, jnp.log2(1.0 + jnp.exp2(jnp.minimum(z, EXP2_CLAMP))))


def _lane_chunks(x):
    return [x[:, c:c + LANES] for c in range(0, x.shape[1], LANES)]


def _diff_update(idx, s, v_aug, dm_ref, dacc_ref):
    m_old = dm_ref[idx]
    m_new = jnp.maximum(m_old, jnp.max(s, axis=-1, keepdims=True))
    alpha = jnp.exp2(m_old - m_new)
    p = jnp.concatenate([jnp.exp2(sc - m_new) for sc in _lane_chunks(s)], axis=1)
    pv = jnp.dot(p.astype(jnp.bfloat16), v_aug, preferred_element_type=jnp.float32)
    for c in (0, LANES):
        dacc_ref[idx, :, c:c + LANES] = alpha * dacc_ref[idx, :, c:c + LANES] + pv[:, c:c + LANES]
    dm_ref[idx] = m_new


def _diff_block(qdm_ref, k_of, v_of, bias_of, mask, dm_ref, dacc_ref):
    for h in range(N_DIFF_HEADS):
        k, v = k_of(h), v_of(h)
        v_aug = jnp.concatenate([v, jnp.ones_like(v)], axis=1)
        bias = bias_of(h)
        for which in range(2):
            idx = 2 * h + which
            s = lax.dot_general(qdm_ref[idx], k, _NT, preferred_element_type=jnp.float32) + bias
            if mask is not None:
                s = jnp.where(mask, s, NEG)
            _diff_update(idx, s, v_aug, dm_ref, dacc_ref)


def _sb_block(qsm_ref, k_of, v_of, tri, mask, lo_lanes, sc_ref, sacc_ref):
    for pair in range(N_SB_HEADS // 2):
        k, v = k_of(pair), v_of(pair)
        zero = jnp.zeros_like(v)
        v_split = jnp.concatenate([jnp.where(lo_lanes, v, zero), jnp.where(lo_lanes, zero, v)], axis=0)
        a_parts = []
        for half in range(2):
            idx = 2 * pair + half
            z = lax.dot_general(qsm_ref[idx], k, _NT, preferred_element_type=jnp.float32)
            sp = _softplus_log2(z)
            if mask is not None:
                sp = jnp.where(mask, sp, 0.0)
            cum = jnp.dot(sp.astype(jnp.bfloat16), tri, preferred_element_type=jnp.float32)
            c_old = sc_ref[idx]
            a = jnp.concatenate([jnp.exp2(zc - cc - c_old)
                                 for zc, cc in zip(_lane_chunks(z), _lane_chunks(cum))], axis=1)
            if mask is not None:
                a = jnp.where(mask, a, 0.0)
            a_parts.append(a.astype(jnp.bfloat16))
            sc_ref[idx] = c_old + jnp.sum(sp, axis=-1, keepdims=True)
        sacc_ref[pair] += jnp.dot(jnp.concatenate(a_parts, axis=1), v_split,
                                  preferred_element_type=jnp.float32)


def _attn_body(lam_ref, gd_ref, gs_ref, qd_ref, qs_ref, kd_ref, vd_ref, ks_ref, vs_ref,
               kdm_ref, vdm_ref, ksm_ref, vsm_ref, o_ref,
               qdm_ref, qsm_ref, dm_ref, dacc_ref, sc_ref, sacc_ref, tri_ref, *, tq, meta_only):
    j = 0 if meta_only else pl.program_id(1)
    lam = _lambda_value(lam_ref)
    lane = lax.broadcasted_iota(jnp.int32, (1, LANES), 1)
    lo_lanes = lane < DIFF_QK_DIM
    tri_r = lax.broadcasted_iota(jnp.int32, (TK, TK), 0)
    tri_c = lax.broadcasted_iota(jnp.int32, (TK, TK), 1)
    tri_ref[...] = jnp.where(tri_r >= tri_c, 1.0, 0.0).astype(jnp.bfloat16)

    for src, dst in ((qd_ref, qdm_ref), (qs_ref, qsm_ref)):
        for h in range(4):
            q = src[0, :, h * LANES:(h + 1) * LANES]
            zero = jnp.zeros_like(q)
            dst[2 * h] = jnp.where(lo_lanes, q, zero)
            dst[2 * h + 1] = jnp.where(lo_lanes, zero, q)
    dm_ref[...] = jnp.full(dm_ref.shape, NEG, jnp.float32)
    dacc_ref[...] = jnp.zeros(dacc_ref.shape, jnp.float32)
    sc_ref[...] = jnp.zeros(sc_ref.shape, jnp.float32)
    sacc_ref[...] = jnp.zeros(sacc_ref.shape, jnp.float32)

    q0 = 0 if meta_only else N_META + j * tq
    meta_pos = lax.broadcasted_iota(jnp.int32, (1, META_ROWS), 1)
    blk_pos = lax.broadcasted_iota(jnp.int32, (1, TK), 1)
    mrow = lax.broadcasted_iota(jnp.int32, (tq, META_ROWS), 0)
    mcol = lax.broadcasted_iota(jnp.int32, (tq, META_ROWS), 1)
    meta_valid = mcol < N_META
    head_cols = lambda h: slice(h * LANES, (h + 1) * LANES)

    def real_block(ref, start):
        return lambda h: ref[0, pl.ds(start, TK), head_cols(h)]

    def meta_block(ref):
        return lambda h: ref[:, head_cols(h)]

    def diff_meta(mask):
        _diff_block(qdm_ref, meta_block(kdm_ref), meta_block(vdm_ref),
                    lambda h: _alibi_slope(h) * LOG2E * (meta_pos - q0).astype(jnp.float32),
                    mask, dm_ref, dacc_ref)

    def sb_meta(mask):
        _sb_block(qsm_ref, meta_block(ksm_ref), meta_block(vsm_ref),
                  tri_ref[0:META_ROWS, 0:META_ROWS], mask, lo_lanes, sc_ref, sacc_ref)

    if meta_only:
        diff_meta(meta_valid & (mcol <= mrow))
        sb_meta(meta_valid & (mcol < mrow))
    else:
        row = lax.broadcasted_iota(jnp.int32, (tq, TK), 0)
        col = lax.broadcasted_iota(jnp.int32, (tq, TK), 1)
        diag = pl.multiple_of(j * TK, TK)
        diff_meta(meta_valid)
        _sb_block(qsm_ref, real_block(ks_ref, diag), real_block(vs_ref, diag),
                  tri_ref[...], col < row, lo_lanes, sc_ref, sacc_ref)

        def mid(t, carry):
            d_start = pl.multiple_of(t * TK, TK)
            s_start = pl.multiple_of((j - 1 - t) * TK, TK)
            _diff_block(qdm_ref, real_block(kd_ref, d_start), real_block(vd_ref, d_start),
                        lambda h: _alibi_slope(h) * LOG2E * ((t - j) * TK + blk_pos).astype(jnp.float32),
                        None, dm_ref, dacc_ref)
            _sb_block(qsm_ref, real_block(ks_ref, s_start), real_block(vs_ref, s_start),
                      tri_ref[...], None, lo_lanes, sc_ref, sacc_ref)
            return carry
        lax.fori_loop(0, j, mid, 0)

        _diff_block(qdm_ref, real_block(kd_ref, diag), real_block(vd_ref, diag),
                    lambda h: _alibi_slope(h) * LOG2E * blk_pos.astype(jnp.float32),
                    col <= row, dm_ref, dacc_ref)
        sb_meta(meta_valid)

    for h in range(N_DIFF_HEADS):
        o = (dacc_ref[2 * h, :, 0:LANES] / dacc_ref[2 * h, :, LANES:2 * LANES]
             - lam * (dacc_ref[2 * h + 1, :, 0:LANES] / dacc_ref[2 * h + 1, :, LANES:2 * LANES]))
        o = o * _rms_scale(o) * gd_ref[...] * (1.0 - LAMBDA_INIT)
        o_ref[0, :, head_cols(h)] = o.astype(o_ref.dtype)
    for pair in range(N_SB_HEADS // 2):
        o = sacc_ref[pair]
        sq = o * o
        ms_lo = jnp.sum(jnp.where(lo_lanes, sq, 0.0), axis=-1, keepdims=True)
        ms_hi = jnp.sum(jnp.where(lo_lanes, 0.0, sq), axis=-1, keepdims=True)
        ms = jnp.where(lo_lanes, ms_lo, ms_hi) * (1.0 / SB_HEAD_DIM)
        o = o * lax.rsqrt(ms + EPS) * gs_ref[...]
        o_ref[0, :, GROUP_WIDTH + pair * LANES:GROUP_WIDTH + (pair + 1) * LANES] = o.astype(o_ref.dtype)


def _attn_scratch(tq):
    maps = 2 * N_DIFF_HEADS
    return [pltpu.VMEM((maps, tq, LANES), jnp.bfloat16),
            pltpu.VMEM((N_SB_HEADS, tq, LANES), jnp.bfloat16),
            pltpu.VMEM((maps, tq, LANES), jnp.float32),
            pltpu.VMEM((maps, tq, 2 * LANES), jnp.float32),
            pltpu.VMEM((N_SB_HEADS, tq, LANES), jnp.float32),
            pltpu.VMEM((N_SB_HEADS // 2, tq, LANES), jnp.float32),
            pltpu.VMEM((TK, TK), jnp.bfloat16)]


def _meta_kv_specs(index_map_for):
    return [pl.BlockSpec((META_ROWS, GROUP_WIDTH), index_map_for(g)) for g in (1, 2, 4, 5)]


def _attn(proj, proj_meta, lam, gd, gs, batch, seq):
    proj3 = proj.reshape(batch, seq, PROJ_WIDTH)
    small = [pl.BlockSpec(lam.shape, lambda b, j: (0, 0)),
             pl.BlockSpec(gd.shape, lambda b, j: (0, 0)),
             pl.BlockSpec(gs.shape, lambda b, j: (0, 0))]
    qspec = lambda g: pl.BlockSpec((1, TQ, GROUP_WIDTH), lambda b, j, g=g: (b, j, g))
    kvspec = lambda g: pl.BlockSpec((1, seq, GROUP_WIDTH), lambda b, j, g=g: (b, 0, g))
    meta_specs = _meta_kv_specs(lambda g: (lambda b, j, g=g: (0, g)))
    out = pl.pallas_call(
        functools.partial(_attn_body, tq=TQ, meta_only=False),
        grid=(batch, seq // TQ),
        in_specs=small + [qspec(0), qspec(3), kvspec(1), kvspec(2), kvspec(4), kvspec(5)] + meta_specs,
        out_specs=pl.BlockSpec((1, TQ, D_MODEL), lambda b, j: (b, j, 0)),
        out_shape=jax.ShapeDtypeStruct((batch, seq, D_MODEL), jnp.bfloat16),
        scratch_shapes=_attn_scratch(TQ),
        compiler_params=_params(2),
        name="attn",
    )(lam, gd, gs, proj3, proj3, proj3, proj3, proj3, proj3,
      proj_meta, proj_meta, proj_meta, proj_meta)
    return out.reshape(batch * seq, D_MODEL)


def _attn_meta_body(lam_ref, gd_ref, gs_ref, qd_ref, qs_ref, kdm_ref, vdm_ref, ksm_ref, vsm_ref,
                    o_ref, *scratch):
    _attn_body(lam_ref, gd_ref, gs_ref, qd_ref, qs_ref, None, None, None, None,
               kdm_ref, vdm_ref, ksm_ref, vsm_ref, o_ref, *scratch,
               tq=META_ROWS, meta_only=True)


def _attn_meta(proj_meta, lam, gd, gs):
    proj3 = proj_meta.reshape(1, META_ROWS, PROJ_WIDTH)
    small = [pl.BlockSpec(lam.shape, lambda i: (0, 0)),
             pl.BlockSpec(gd.shape, lambda i: (0, 0)),
             pl.BlockSpec(gs.shape, lambda i: (0, 0))]
    qspec = lambda g: pl.BlockSpec((1, META_ROWS, GROUP_WIDTH), lambda i, g=g: (0, 0, g))
    meta_specs = _meta_kv_specs(lambda g: (lambda i, g=g: (0, g)))
    out = pl.pallas_call(
        _attn_meta_body,
        grid=(1,),
        in_specs=small + [qspec(0), qspec(3)] + meta_specs,
        out_specs=pl.BlockSpec((1, META_ROWS, D_MODEL), lambda i: (0, 0, 0)),
        out_shape=jax.ShapeDtypeStruct((1, META_ROWS, D_MODEL), jnp.bfloat16),
        scratch_shapes=_attn_scratch(META_ROWS),
        compiler_params=_params(1),
        name="attn_meta",
    )(lam, gd, gs, proj3, proj3, proj_meta, proj_meta, proj_meta, proj_meta)
    return out.reshape(META_ROWS, D_MODEL)


def _outproj_kernel(x_ref, a_ref, w_ref, o_ref):
    o_ref[...] = x_ref[...] + jnp.dot(a_ref[...], w_ref[...], preferred_element_type=jnp.float32)


def _outproj(x_rows, merged, w, tm):
    n = x_rows.shape[0]
    return pl.pallas_call(
        _outproj_kernel,
        grid=(n // tm,),
        in_specs=[pl.BlockSpec((tm, D_MODEL), lambda i: (i, 0)),
                  pl.BlockSpec((tm, D_MODEL), lambda i: (i, 0)),
                  pl.BlockSpec((D_MODEL, D_MODEL), lambda i: (0, 0))],
        out_specs=pl.BlockSpec((tm, D_MODEL), lambda i: (i, 0)),
        out_shape=jax.ShapeDtypeStruct((n, D_MODEL), jnp.float32),
        compiler_params=_params(1),
        name="outproj",
    )(x_rows, merged, w)


def _meta_up_kernel(h_ref, g_ref, w_ref, o_ref):
    h = h_ref[...]
    xn = (h * _rms_scale(h) * g_ref[...]).astype(jnp.bfloat16)
    for c in range(2 * D_FF // FF_CHUNK):
        cols = slice(c * FF_CHUNK, (c + 1) * FF_CHUNK)
        o_ref[:, cols] = jnp.dot(xn, w_ref[:, cols], preferred_element_type=jnp.float32)


def _meta_up(h_meta, g, w):
    return pl.pallas_call(
        _meta_up_kernel,
        grid=(1,),
        in_specs=[pl.BlockSpec((META_ROWS, D_MODEL), lambda i: (0, 0)),
                  pl.BlockSpec((1, D_MODEL), lambda i: (0, 0)),
                  pl.BlockSpec((D_MODEL, 2 * D_FF), lambda i: (0, 0))],
        out_specs=pl.BlockSpec((META_ROWS, 2 * D_FF), lambda i: (0, 0)),
        out_shape=jax.ShapeDtypeStruct((META_ROWS, 2 * D_FF), jnp.float32),
        compiler_params=_params(1),
        name="meta_up",
    )(h_meta, g, w)


def _ffn_kernel(x_ref, a_ref, wo_ref, g_ref, wu_ref, cw_ref, cb_ref, halo0_ref, wd_ref, gf_ref, o_ref,
                halo_ref, act_ref, *, tiles_per_seq):
    i = pl.program_id(0)

    @pl.when(i % tiles_per_seq == 0)
    def _():
        halo_ref[...] = halo0_ref[...]

    tm = x_ref.shape[0]
    h1 = x_ref[...] + jnp.dot(a_ref[...], wo_ref[...], preferred_element_type=jnp.float32)
    xn = (h1 * _rms_scale(h1) * g_ref[...]).astype(jnp.bfloat16)
    row = lax.broadcasted_iota(jnp.int32, (tm, FF_CHUNK), 0)

    def conv(cols):
        u = jnp.dot(xn, wu_ref[:, cols], preferred_element_type=jnp.float32)
        prev2 = halo_ref[HALO_ROWS - 2:HALO_ROWS - 1, cols]
        prev1 = halo_ref[HALO_ROWS - 1:HALO_ROWS, cols]
        u1 = jnp.where(row == 0, prev1, pltpu.roll(u, 1, 0))
        u2 = jnp.where(row == 0, prev2, jnp.where(row == 1, prev1, pltpu.roll(u, 2, 0)))
        halo_ref[:, cols] = u[tm - HALO_ROWS:, :]
        return cb_ref[:, cols] + cw_ref[0:1, cols] * u2 + cw_ref[1:2, cols] * u1 + cw_ref[2:3, cols] * u

    y = h1
    n_chunks = D_FF // FF_CHUNK
    for c0 in range(0, n_chunks, FF_DOWN_GROUP):
        c1 = min(c0 + FF_DOWN_GROUP, n_chunks)
        for c in range(c0, c1):
            gate = conv(slice(c * FF_CHUNK, (c + 1) * FF_CHUNK))
            up = conv(slice(D_FF + c * FF_CHUNK, D_FF + (c + 1) * FF_CHUNK))
            act = gate * (1.0 / (1.0 + jnp.exp(-gate))) * up
            act_ref[:, c * FF_CHUNK:(c + 1) * FF_CHUNK] = act.astype(act_ref.dtype)
        ks = slice(c0 * FF_CHUNK, c1 * FF_CHUNK)
        y = y + jnp.dot(act_ref[:, ks], wd_ref[ks, :], preferred_element_type=jnp.float32)
    o_ref[...] = y * _rms_scale(y) * gf_ref[...]


def _ffn(x_rows, merged, wo, g, wu, cw, cb, halo0, wd, gf, tm, seq):
    n = x_rows.shape[0]
    const = lambda shape: pl.BlockSpec(shape, lambda i: (0, 0))
    return pl.pallas_call(
        functools.partial(_ffn_kernel, tiles_per_seq=seq // tm),
        grid=(n // tm,),
        in_specs=[pl.BlockSpec((tm, D_MODEL), lambda i: (i, 0)),
                  pl.BlockSpec((tm, D_MODEL), lambda i: (i, 0)),
                  const((D_MODEL, D_MODEL)),
                  const((1, D_MODEL)),
                  const((D_MODEL, 2 * D_FF)),
                  const((3, 2 * D_FF)),
                  const((1, 2 * D_FF)),
                  const((HALO_ROWS, 2 * D_FF)),
                  const((D_FF, D_MODEL)),
                  const((1, D_MODEL))],
        out_specs=pl.BlockSpec((tm, D_MODEL), lambda i: (i, 0)),
        out_shape=jax.ShapeDtypeStruct((n, D_MODEL), jnp.float32),
        scratch_shapes=[pltpu.VMEM((HALO_ROWS, 2 * D_FF), jnp.float32),
                        pltpu.VMEM((tm, D_FF), jnp.bfloat16)],
        compiler_params=_params(1),
        name="ffn",
    )(x_rows, merged, wo, g, wu, cw, cb, halo0, wd, gf)


def kernel(x, meta_tokens, g_attn, w_in, lam_q1, lam_k1, lam_q2, lam_k2, g_diff, g_sb, w_out, g_ffn,
           w_up, conv_w, conv_b, w_down, g_final):
    batch, seq, _ = x.shape
    bf16 = jnp.bfloat16
    qscale = jnp.concatenate([jnp.full((GROUP_WIDTH,), DIFF_QK_DIM ** -0.5 * LOG2E, jnp.float32),
                              jnp.ones((2 * GROUP_WIDTH,), jnp.float32),
                              jnp.full((GROUP_WIDTH,), SB_HEAD_DIM ** -0.5 * LOG2E, jnp.float32),
                              jnp.ones((2 * GROUP_WIDTH,), jnp.float32)])
    w_in_b = (w_in[0] * qscale).astype(bf16)
    w_out_b = w_out[0].astype(bf16)
    w_up_b = w_up[0].astype(bf16)
    w_down_b = w_down[0].astype(bf16)
    lam = jnp.stack([lam_q1[0], lam_k1[0], lam_q2[0], lam_k2[0]]).astype(jnp.float32)
    gd = g_diff[0].reshape(1, DIFF_V_DIM)
    gs = jnp.tile(g_sb[0], 2).reshape(1, LANES)
    g_attn2, g_ffn2, g_final2 = g_attn[0].reshape(1, -1), g_ffn[0].reshape(1, -1), g_final.reshape(1, -1)

    x_rows = x.reshape(batch * seq, D_MODEL)
    meta_rows = jnp.pad(meta_tokens.astype(x.dtype), ((0, META_ROWS - N_META), (0, 0)))

    proj_meta = _inproj(meta_rows, g_attn2, w_in_b, META_ROWS)
    merged_meta = _attn_meta(proj_meta, lam, gd, gs)
    h1_meta = _outproj(meta_rows, merged_meta, w_out_b, META_ROWS)
    u_meta = _meta_up(h1_meta, g_ffn2, w_up_b)
    halo0 = jnp.pad(u_meta[N_META - 2:N_META], ((HALO_ROWS - 2, 0), (0, 0)))

    proj = _inproj(x_rows, g_attn2, w_in_b, 512)
    merged = _attn(proj, proj_meta, lam, gd, gs, batch, seq)
    out = _ffn(x_rows, merged, w_out_b, g_ffn2, w_up_b, conv_w[0], conv_b[0].reshape(1, -1), halo0,
               w_down_b, g_final2, 256, seq)
    return out.reshape(batch, seq, D_MODEL)
```

```python
import functools
import math

import jax
import jax.numpy as jnp
from jax import lax
from jax.experimental import pallas as pl
from jax.experimental.pallas import tpu as pltpu

D_MODEL = 1024
N_META = 16
N_DIFF_HEADS = 4
DIFF_QK_DIM = 64
DIFF_V_DIM = 128
N_SB_HEADS = 8
SB_HEAD_DIM = 64
GROUP_WIDTH = 512
PROJ_WIDTH = 6 * GROUP_WIDTH
D_FF = 2816
EPS = 1e-6
LAMBDA_INIT = 0.8 - 0.6 * math.exp(-0.3 * 0)
NEG = -1e30
LOG2E = math.log2(math.e)
EXP2_CLAMP = 126.0

LANES = 128
META_ROWS = 128
TQ = 256
TK = 256
FF_CHUNK = 256
HALO_ROWS = 8

VMEM_LIMIT = 56 * 1024 * 1024

_NT = (((1,), (1,)), ((), ()))


def _alibi_slope(h):
    start = 2.0 ** (-8.0 / N_DIFF_HEADS)
    return start ** (h + 1)


def _params(n_grid):
    return pltpu.CompilerParams(dimension_semantics=("arbitrary",) * n_grid,
                                vmem_limit_bytes=VMEM_LIMIT)


def _rms_scale(x):
    return lax.rsqrt(jnp.mean(x * x, axis=-1, keepdims=True) + EPS)


def _inproj_kernel(x_ref, g_ref, w_ref, o_ref):
    x = x_ref[...]
    xn = (x * _rms_scale(x) * g_ref[...]).astype(jnp.bfloat16)
    for c in range(PROJ_WIDTH // GROUP_WIDTH):
        cols = slice(c * GROUP_WIDTH, (c + 1) * GROUP_WIDTH)
        o_ref[:, cols] = jnp.dot(xn, w_ref[:, cols],
                                 preferred_element_type=jnp.float32).astype(jnp.bfloat16)


def _inproj(rows, g, w, tm):
    n = rows.shape[0]
    return pl.pallas_call(
        _inproj_kernel,
        grid=(n // tm,),
        in_specs=[pl.BlockSpec((tm, D_MODEL), lambda i: (i, 0)),
                  pl.BlockSpec((1, D_MODEL), lambda i: (0, 0)),
                  pl.BlockSpec((D_MODEL, PROJ_WIDTH), lambda i: (0, 0))],
        out_specs=pl.BlockSpec((tm, PROJ_WIDTH), lambda i: (i, 0)),
        out_shape=jax.ShapeDtypeStruct((n, PROJ_WIDTH), jnp.bfloat16),
        compiler_params=_params(1),
        name="inproj",
    )(rows, g, w)


def _lambda_value(lam_ref):
    lam = lam_ref[...]
    s1 = jnp.sum(lam[0:1] * lam[1:2], axis=-1, keepdims=True)
    s2 = jnp.sum(lam[2:3] * lam[3:4], axis=-1, keepdims=True)
    return jnp.exp(s1) - jnp.exp(s2) + LAMBDA_INIT


def _softplus_log2(z):
    return jnp.maximum(z, jnp.log2(1.0 + jnp.exp2(jnp.minimum(z, EXP2_CLAMP))))


def _lane_chunks(x):
    return [x[:, c:c + LANES] for c in range(0, x.shape[1], LANES)]


def _diff_update(idx, s, v_aug, dm_ref, dacc_ref, first):
    row_max = jnp.max(s, axis=-1, keepdims=True)
    if first:
        m_new = jnp.broadcast_to(row_max, dm_ref.shape[1:])
    else:
        m_old = dm_ref[idx]
        m_new = jnp.maximum(m_old, row_max)
        alpha = jnp.exp2(m_old - m_new)
    p = jnp.concatenate([jnp.exp2(sc - m_new) for sc in _lane_chunks(s)], axis=1)
    pv = jnp.dot(p.astype(jnp.bfloat16), v_aug, preferred_element_type=jnp.float32)
    for c in (0, LANES):
        if first:
            dacc_ref[idx, :, c:c + LANES] = pv[:, c:c + LANES]
        else:
            dacc_ref[idx, :, c:c + LANES] = alpha * dacc_ref[idx, :, c:c + LANES] + pv[:, c:c + LANES]
    dm_ref[idx] = m_new


def _diff_block(qdm_ref, k_of, v_of, bias_of, mask, dm_ref, dacc_ref, first=False):
    for h in range(N_DIFF_HEADS):
        k, v = k_of(h), v_of(h)
        v_aug = jnp.concatenate([v, jnp.ones_like(v)], axis=1)
        bias = bias_of(h)
        for which in range(2):
            idx = 2 * h + which
            s = lax.dot_general(qdm_ref[idx], k, _NT, preferred_element_type=jnp.float32) + bias
            if mask is not None:
                s = jnp.where(mask, s, NEG)
            _diff_update(idx, s, v_aug, dm_ref, dacc_ref, first)


def _sb_block(qsm_ref, k_of, v_of, tri, mask, lo_lanes, sc_ref, sacc_ref, first=False):
    for pair in range(N_SB_HEADS // 2):
        k, v = k_of(pair), v_of(pair)
        zero = jnp.zeros_like(v)
        v_split = jnp.concatenate([jnp.where(lo_lanes, v, zero), jnp.where(lo_lanes, zero, v)], axis=0)
        a_parts = []
        for half in range(2):
            idx = 2 * pair + half
            z = lax.dot_general(qsm_ref[idx], k, _NT, preferred_element_type=jnp.float32)
            sp = _softplus_log2(z)
            if mask is not None:
                sp = jnp.where(mask, sp, 0.0)
            cum = jnp.dot(sp.astype(jnp.bfloat16), tri, preferred_element_type=jnp.float32)
            row_sum = jnp.sum(sp, axis=-1, keepdims=True)
            if first:
                log_a = [zc - cc for zc, cc in zip(_lane_chunks(z), _lane_chunks(cum))]
                sc_ref[idx] = jnp.broadcast_to(row_sum, sc_ref.shape[1:])
            else:
                c_old = sc_ref[idx]
                log_a = [zc - cc - c_old for zc, cc in zip(_lane_chunks(z), _lane_chunks(cum))]
                sc_ref[idx] = c_old + row_sum
            a = jnp.concatenate([jnp.exp2(la) for la in log_a], axis=1)
            if mask is not None:
                a = jnp.where(mask, a, 0.0)
            a_parts.append(a.astype(jnp.bfloat16))
        o = jnp.dot(jnp.concatenate(a_parts, axis=1), v_split, preferred_element_type=jnp.float32)
        if first:
            sacc_ref[pair] = o
        else:
            sacc_ref[pair] += o


def _attn_body(lam_ref, gd_ref, gs_ref, qd_ref, qs_ref, kd_ref, vd_ref, ks_ref, vs_ref,
               kdm_ref, vdm_ref, ksm_ref, vsm_ref, o_ref,
               qdm_ref, qsm_ref, dm_ref, dacc_ref, sc_ref, sacc_ref, tri_ref, *, tq, meta_only):
    j = 0 if meta_only else pl.program_id(1)
    lam = _lambda_value(lam_ref)
    lane = lax.broadcasted_iota(jnp.int32, (1, LANES), 1)
    lo_lanes = lane < DIFF_QK_DIM
    tri_r = lax.broadcasted_iota(jnp.int32, (TK, TK), 0)
    tri_c = lax.broadcasted_iota(jnp.int32, (TK, TK), 1)
    tri_ref[...] = jnp.where(tri_r >= tri_c, 1.0, 0.0).astype(jnp.bfloat16)

    for src, dst in ((qd_ref, qdm_ref), (qs_ref, qsm_ref)):
        for h in range(4):
            q = src[0, :, h * LANES:(h + 1) * LANES]
            zero = jnp.zeros_like(q)
            dst[2 * h] = jnp.where(lo_lanes, q, zero)
            dst[2 * h + 1] = jnp.where(lo_lanes, zero, q)

    q0 = 0 if meta_only else N_META + j * tq
    meta_pos = lax.broadcasted_iota(jnp.int32, (1, META_ROWS), 1)
    blk_pos = lax.broadcasted_iota(jnp.int32, (1, TK), 1)
    mrow = lax.broadcasted_iota(jnp.int32, (tq, META_ROWS), 0)
    mcol = lax.broadcasted_iota(jnp.int32, (tq, META_ROWS), 1)
    meta_valid = mcol < N_META
    head_cols = lambda h: slice(h * LANES, (h + 1) * LANES)

    def real_block(ref, start):
        return lambda h: ref[0, pl.ds(start, TK), head_cols(h)]

    def meta_block(ref):
        return lambda h: ref[:, head_cols(h)]

    def diff_meta(mask):
        _diff_block(qdm_ref, meta_block(kdm_ref), meta_block(vdm_ref),
                    lambda h: _alibi_slope(h) * LOG2E * (meta_pos - q0).astype(jnp.float32),
                    mask, dm_ref, dacc_ref, first=True)

    def sb_meta(mask, first):
        _sb_block(qsm_ref, meta_block(ksm_ref), meta_block(vsm_ref),
                  tri_ref[0:META_ROWS, 0:META_ROWS], mask, lo_lanes, sc_ref, sacc_ref, first=first)

    if meta_only:
        diff_meta(meta_valid & (mcol <= mrow))
        sb_meta(meta_valid & (mcol < mrow), first=True)
    else:
        row = lax.broadcasted_iota(jnp.int32, (tq, TK), 0)
        col = lax.broadcasted_iota(jnp.int32, (tq, TK), 1)
        diag = pl.multiple_of(j * TK, TK)
        diff_meta(meta_valid)
        _sb_block(qsm_ref, real_block(ks_ref, diag), real_block(vs_ref, diag),
                  tri_ref[...], col < row, lo_lanes, sc_ref, sacc_ref, first=True)

        def mid(t, carry):
            d_start = pl.multiple_of(t * TK, TK)
            s_start = pl.multiple_of((j - 1 - t) * TK, TK)
            _diff_block(qdm_ref, real_block(kd_ref, d_start), real_block(vd_ref, d_start),
                        lambda h: _alibi_slope(h) * LOG2E * ((t - j) * TK + blk_pos).astype(jnp.float32),
                        None, dm_ref, dacc_ref)
            _sb_block(qsm_ref, real_block(ks_ref, s_start), real_block(vs_ref, s_start),
                      tri_ref[...], None, lo_lanes, sc_ref, sacc_ref)
            return carry
        lax.fori_loop(0, j, mid, 0)

        _diff_block(qdm_ref, real_block(kd_ref, diag), real_block(vd_ref, diag),
                    lambda h: _alibi_slope(h) * LOG2E * blk_pos.astype(jnp.float32),
                    col <= row, dm_ref, dacc_ref)
        sb_meta(meta_valid, first=False)

    for h in range(N_DIFF_HEADS):
        o = (dacc_ref[2 * h, :, 0:LANES] / dacc_ref[2 * h, :, LANES:2 * LANES]
             - lam * (dacc_ref[2 * h + 1, :, 0:LANES] / dacc_ref[2 * h + 1, :, LANES:2 * LANES]))
        o = o * _rms_scale(o) * gd_ref[...] * (1.0 - LAMBDA_INIT)
        o_ref[0, :, head_cols(h)] = o.astype(o_ref.dtype)
    for pair in range(N_SB_HEADS // 2):
        o = sacc_ref[pair]
        sq = o * o
        ms_lo = jnp.sum(jnp.where(lo_lanes, sq, 0.0), axis=-1, keepdims=True)
        ms_hi = jnp.sum(jnp.where(lo_lanes, 0.0, sq), axis=-1, keepdims=True)
        ms = jnp.where(lo_lanes, ms_lo, ms_hi) * (1.0 / SB_HEAD_DIM)
        o = o * lax.rsqrt(ms + EPS) * gs_ref[...]
        o_ref[0, :, GROUP_WIDTH + pair * LANES:GROUP_WIDTH + (pair + 1) * LANES] = o.astype(o_ref.dtype)


def _attn_scratch(tq):
    maps = 2 * N_DIFF_HEADS
    return [pltpu.VMEM((maps, tq, LANES), jnp.bfloat16),
            pltpu.VMEM((N_SB_HEADS, tq, LANES), jnp.bfloat16),
            pltpu.VMEM((maps, tq, LANES), jnp.float32),
            pltpu.VMEM((maps, tq, 2 * LANES), jnp.float32),
            pltpu.VMEM((N_SB_HEADS, tq, LANES), jnp.float32),
            pltpu.VMEM((N_SB_HEADS // 2, tq, LANES), jnp.float32),
            pltpu.VMEM((TK, TK), jnp.bfloat16)]


def _meta_kv_specs(index_map_for):
    return [pl.BlockSpec((META_ROWS, GROUP_WIDTH), index_map_for(g)) for g in (1, 2, 4, 5)]


def _attn(proj, proj_meta, lam, gd, gs, batch, seq):
    proj3 = proj.reshape(batch, seq, PROJ_WIDTH)
    small = [pl.BlockSpec(lam.shape, lambda b, j: (0, 0)),
             pl.BlockSpec(gd.shape, lambda b, j: (0, 0)),
             pl.BlockSpec(gs.shape, lambda b, j: (0, 0))]
    qspec = lambda g: pl.BlockSpec((1, TQ, GROUP_WIDTH), lambda b, j, g=g: (b, j, g))
    kvspec = lambda g: pl.BlockSpec((1, seq, GROUP_WIDTH), lambda b, j, g=g: (b, 0, g))
    meta_specs = _meta_kv_specs(lambda g: (lambda b, j, g=g: (0, g)))
    out = pl.pallas_call(
        functools.partial(_attn_body, tq=TQ, meta_only=False),
        grid=(batch, seq // TQ),
        in_specs=small + [qspec(0), qspec(3), kvspec(1), kvspec(2), kvspec(4), kvspec(5)] + meta_specs,
        out_specs=pl.BlockSpec((1, TQ, D_MODEL), lambda b, j: (b, j, 0)),
        out_shape=jax.ShapeDtypeStruct((batch, seq, D_MODEL), jnp.bfloat16),
        scratch_shapes=_attn_scratch(TQ),
        compiler_params=_params(2),
        name="attn",
    )(lam, gd, gs, proj3, proj3, proj3, proj3, proj3, proj3,
      proj_meta, proj_meta, proj_meta, proj_meta)
    return out.reshape(batch * seq, D_MODEL)


def _attn_meta_body(lam_ref, gd_ref, gs_ref, qd_ref, qs_ref, kdm_ref, vdm_ref, ksm_ref, vsm_ref,
                    o_ref, *scratch):
    _attn_body(lam_ref, gd_ref, gs_ref, qd_ref, qs_ref, None, None, None, None,
               kdm_ref, vdm_ref, ksm_ref, vsm_ref, o_ref, *scratch,
               tq=META_ROWS, meta_only=True)


def _attn_meta(proj_meta, lam, gd, gs):
    proj3 = proj_meta.reshape(1, META_ROWS, PROJ_WIDTH)
    small = [pl.BlockSpec(lam.shape, lambda i: (0, 0)),
             pl.BlockSpec(gd.shape, lambda i: (0, 0)),
             pl.BlockSpec(gs.shape, lambda i: (0, 0))]
    qspec = lambda g: pl.BlockSpec((1, META_ROWS, GROUP_WIDTH), lambda i, g=g: (0, 0, g))
    meta_specs = _meta_kv_specs(lambda g: (lambda i, g=g: (0, g)))
    out = pl.pallas_call(
        _attn_meta_body,
        grid=(1,),
        in_specs=small + [qspec(0), qspec(3)] + meta_specs,
        out_specs=pl.BlockSpec((1, META_ROWS, D_MODEL), lambda i: (0, 0, 0)),
        out_shape=jax.ShapeDtypeStruct((1, META_ROWS, D_MODEL), jnp.bfloat16),
        scratch_shapes=_attn_scratch(META_ROWS),
        compiler_params=_params(1),
        name="attn_meta",
    )(lam, gd, gs, proj3, proj3, proj_meta, proj_meta, proj_meta, proj_meta)
    return out.reshape(META_ROWS, D_MODEL)


def _outproj_kernel(x_ref, a_ref, w_ref, o_ref):
    o_ref[...] = x_ref[...] + jnp.dot(a_ref[...], w_ref[...], preferred_element_type=jnp.float32)


def _outproj(x_rows, merged, w, tm):
    n = x_rows.shape[0]
    return pl.pallas_call(
        _outproj_kernel,
        grid=(n // tm,),
        in_specs=[pl.BlockSpec((tm, D_MODEL), lambda i: (i, 0)),
                  pl.BlockSpec((tm, D_MODEL), lambda i: (i, 0)),
                  pl.BlockSpec((D_MODEL, D_MODEL), lambda i: (0, 0))],
        out_specs=pl.BlockSpec((tm, D_MODEL), lambda i: (i, 0)),
        out_shape=jax.ShapeDtypeStruct((n, D_MODEL), jnp.float32),
        compiler_params=_params(1),
        name="outproj",
    )(x_rows, merged, w)


def _meta_up_kernel(h_ref, g_ref, w_ref, o_ref):
    h = h_ref[...]
    xn = (h * _rms_scale(h) * g_ref[...]).astype(jnp.bfloat16)
    for c in range(2 * D_FF // FF_CHUNK):
        cols = slice(c * FF_CHUNK, (c + 1) * FF_CHUNK)
        o_ref[:, cols] = jnp.dot(xn, w_ref[:, cols], preferred_element_type=jnp.float32)


def _meta_up(h_meta, g, w):
    return pl.pallas_call(
        _meta_up_kernel,
        grid=(1,),
        in_specs=[pl.BlockSpec((META_ROWS, D_MODEL), lambda i: (0, 0)),
                  pl.BlockSpec((1, D_MODEL), lambda i: (0, 0)),
                  pl.BlockSpec((D_MODEL, 2 * D_FF), lambda i: (0, 0))],
        out_specs=pl.BlockSpec((META_ROWS, 2 * D_FF), lambda i: (0, 0)),
        out_shape=jax.ShapeDtypeStruct((META_ROWS, 2 * D_FF), jnp.float32),
        compiler_params=_params(1),
        name="meta_up",
    )(h_meta, g, w)


def _ffn_kernel(x_ref, a_ref, wo_ref, g_ref, wu_ref, cw_ref, cb_ref, halo0_ref, wd_ref, gf_ref, o_ref,
                halo_ref, act_ref, *, tiles_per_seq):
    i = pl.program_id(0)

    @pl.when(i % tiles_per_seq == 0)
    def _():
        halo_ref[...] = halo0_ref[...]

    tm = x_ref.shape[0]
    h1 = x_ref[...] + jnp.dot(a_ref[...], wo_ref[...], preferred_element_type=jnp.float32)
    xn = (h1 * _rms_scale(h1) * g_ref[...]).astype(jnp.bfloat16)
    row = lax.broadcasted_iota(jnp.int32, (tm, FF_CHUNK), 0)

    def conv(cols):
        u = jnp.dot(xn, wu_ref[:, cols], preferred_element_type=jnp.float32)
        prev2 = halo_ref[HALO_ROWS - 2:HALO_ROWS - 1, cols]
        prev1 = halo_ref[HALO_ROWS - 1:HALO_ROWS, cols]
        u1 = jnp.where(row == 0, prev1, pltpu.roll(u, 1, 0))
        u2 = jnp.where(row == 0, prev2, jnp.where(row == 1, prev1, pltpu.roll(u, 2, 0)))
        halo_ref[:, cols] = u[tm - HALO_ROWS:, :]
        return cb_ref[:, cols] + cw_ref[0:1, cols] * u2 + cw_ref[1:2, cols] * u1 + cw_ref[2:3, cols] * u

    for c in range(D_FF // FF_CHUNK):
        gate = conv(slice(c * FF_CHUNK, (c + 1) * FF_CHUNK))
        up = conv(slice(D_FF + c * FF_CHUNK, D_FF + (c + 1) * FF_CHUNK))
        act = gate * (1.0 / (1.0 + jnp.exp(-gate))) * up
        act_ref[:, c * FF_CHUNK:(c + 1) * FF_CHUNK] = act.astype(act_ref.dtype)
    y = h1 + jnp.dot(act_ref[...], wd_ref[...], preferred_element_type=jnp.float32)
    o_ref[...] = y * _rms_scale(y) * gf_ref[...]


def _ffn(x_rows, merged, wo, g, wu, cw, cb, halo0, wd, gf, tm, seq):
    n = x_rows.shape[0]
    const = lambda shape: pl.BlockSpec(shape, lambda i: (0, 0))
    return pl.pallas_call(
        functools.partial(_ffn_kernel, tiles_per_seq=seq // tm),
        grid=(n // tm,),
        in_specs=[pl.BlockSpec((tm, D_MODEL), lambda i: (i, 0)),
                  pl.BlockSpec((tm, D_MODEL), lambda i: (i, 0)),
                  const((D_MODEL, D_MODEL)),
                  const((1, D_MODEL)),
                  const((D_MODEL, 2 * D_FF)),
                  const((3, 2 * D_FF)),
                  const((1, 2 * D_FF)),
                  const((HALO_ROWS, 2 * D_FF)),
                  const((D_FF, D_MODEL)),
                  const((1, D_MODEL))],
        out_specs=pl.BlockSpec((tm, D_MODEL), lambda i: (i, 0)),
        out_shape=jax.ShapeDtypeStruct((n, D_MODEL), jnp.float32),
        scratch_shapes=[pltpu.VMEM((HALO_ROWS, 2 * D_FF), jnp.float32),
                        pltpu.VMEM((tm, D_FF), jnp.bfloat16)],
        compiler_params=_params(1),
        name="ffn",
    )(x_rows, merged, wo, g, wu, cw, cb, halo0, wd, gf)


def kernel(x, meta_tokens, g_attn, w_in, lam_q1, lam_k1, lam_q2, lam_k2, g_diff, g_sb, w_out, g_ffn,
           w_up, conv_w, conv_b, w_down, g_final):
    batch, seq, _ = x.shape
    bf16 = jnp.bfloat16
    qscale = jnp.concatenate([jnp.full((GROUP_WIDTH,), DIFF_QK_DIM ** -0.5 * LOG2E, jnp.float32),
                              jnp.ones((2 * GROUP_WIDTH,), jnp.float32),
                              jnp.full((GROUP_WIDTH,), SB_HEAD_DIM ** -0.5 * LOG2E, jnp.float32),
                              jnp.ones((2 * GROUP_WIDTH,), jnp.float32)])
    w_in_b = (w_in[0] * qscale).astype(bf16)
    w_out_b = w_out[0].astype(bf16)
    w_up_b = w_up[0].astype(bf16)
    w_down_b = w_down[0].astype(bf16)
    lam = jnp.stack([lam_q1[0], lam_k1[0], lam_q2[0], lam_k2[0]]).astype(jnp.float32)
    gd = g_diff[0].reshape(1, DIFF_V_DIM)
    gs = jnp.tile(g_sb[0], 2).reshape(1, LANES)
    g_attn2, g_ffn2, g_final2 = g_attn[0].reshape(1, -1), g_ffn[0].reshape(1, -1), g_final.reshape(1, -1)

    x_rows = x.reshape(batch * seq, D_MODEL)
    meta_rows = jnp.pad(meta_tokens.astype(x.dtype), ((0, META_ROWS - N_META), (0, 0)))

    proj_meta = _inproj(meta_rows, g_attn2, w_in_b, META_ROWS)
    merged_meta = _attn_meta(proj_meta, lam, gd, gs)
    h1_meta = _outproj(meta_rows, merged_meta, w_out_b, META_ROWS)
    u_meta = _meta_up(h1_meta, g_ffn2, w_up_b)
    halo0 = jnp.pad(u_meta[N_META - 2:N_META], ((HALO_ROWS - 2, 0), (0, 0)))

    proj = _inproj(x_rows, g_attn2, w_in_b, 512)
    merged = _attn(proj, proj_meta, lam, gd, gs, batch, seq)
    out = _ffn(x_rows, merged, w_out_b, g_ffn2, w_up_b, conv_w[0], conv_b[0].reshape(1, -1), halo0,
               w_down_b, g_final2, 256, seq)
    return out.reshape(batch, seq, D_MODEL)
```

```python
import functools
import math

import jax
import jax.numpy as jnp
from jax import lax
from jax.experimental import pallas as pl
from jax.experimental.pallas import tpu as pltpu

D_MODEL = 1024
N_META = 16
N_DIFF_HEADS = 4
DIFF_QK_DIM = 64
DIFF_V_DIM = 128
N_SB_HEADS = 8
SB_HEAD_DIM = 64
GROUP_WIDTH = 512
PROJ_WIDTH = 6 * GROUP_WIDTH
D_FF = 2816
EPS = 1e-6
LAMBDA_INIT = 0.8 - 0.6 * math.exp(-0.3 * 0)
NEG = -1e30
LOG2E = math.log2(math.e)
EXP2_CLAMP = 126.0

LANES = 128
META_ROWS = 128
TQ = 256
TK = 256
FF_CHUNK = 256
HALO_ROWS = 8

VMEM_LIMIT = 56 * 1024 * 1024

_NT = (((1,), (1,)), ((), ()))


def _alibi_slope(h):
    start = 2.0 ** (-8.0 / N_DIFF_HEADS)
    return start ** (h + 1)


def _params(n_grid):
    return pltpu.CompilerParams(dimension_semantics=("arbitrary",) * n_grid,
                                vmem_limit_bytes=VMEM_LIMIT)


def _rms_scale(x):
    return lax.rsqrt(jnp.mean(x * x, axis=-1, keepdims=True) + EPS)


def _inproj_kernel(x_ref, g_ref, w_ref, o_ref):
    x = x_ref[...]
    xn = (x * _rms_scale(x) * g_ref[...]).astype(jnp.bfloat16)
    for c in range(PROJ_WIDTH // GROUP_WIDTH):
        cols = slice(c * GROUP_WIDTH, (c + 1) * GROUP_WIDTH)
        o_ref[:, cols] = jnp.dot(xn, w_ref[:, cols],
                                 preferred_element_type=jnp.float32).astype(jnp.bfloat16)


def _inproj(rows, g, w, tm):
    n = rows.shape[0]
    return pl.pallas_call(
        _inproj_kernel,
        grid=(n // tm,),
        in_specs=[pl.BlockSpec((tm, D_MODEL), lambda i: (i, 0)),
                  pl.BlockSpec((1, D_MODEL), lambda i: (0, 0)),
                  pl.BlockSpec((D_MODEL, PROJ_WIDTH), lambda i: (0, 0))],
        out_specs=pl.BlockSpec((tm, PROJ_WIDTH), lambda i: (i, 0)),
        out_shape=jax.ShapeDtypeStruct((n, PROJ_WIDTH), jnp.bfloat16),
        compiler_params=_params(1),
        name="inproj",
    )(rows, g, w)


def _lambda_value(lam_ref):
    lam = lam_ref[...]
    s1 = jnp.sum(lam[0:1] * lam[1:2], axis=-1, keepdims=True)
    s2 = jnp.sum(lam[2:3] * lam[3:4], axis=-1, keepdims=True)
    return jnp.exp(s1) - jnp.exp(s2) + LAMBDA_INIT


def _softplus_log2(z):
    return jnp.maximum(z, jnp.log2(1.0 + jnp.exp2(jnp.minimum(z, EXP2_CLAMP))))


def _lane_chunks(x):
    return [x[:, c:c + LANES] for c in range(0, x.shape[1], LANES)]


def _diff_update(idx, s, v_aug, dm_ref, dacc_ref, first):
    row_max = jnp.max(s, axis=-1, keepdims=True)
    if first:
        m_new = jnp.broadcast_to(row_max, dm_ref.shape[1:])
    else:
        m_old = dm_ref[idx]
        m_new = jnp.maximum(m_old, row_max)
        alpha = jnp.exp2(m_old - m_new)
    p = jnp.concatenate([jnp.exp2(sc - m_new) for sc in _lane_chunks(s)], axis=1)
    pv = jnp.dot(p.astype(jnp.bfloat16), v_aug, preferred_element_type=jnp.float32)
    for c in (0, LANES):
        if first:
            dacc_ref[idx, :, c:c + LANES] = pv[:, c:c + LANES]
        else:
            dacc_ref[idx, :, c:c + LANES] = alpha * dacc_ref[idx, :, c:c + LANES] + pv[:, c:c + LANES]
    dm_ref[idx] = m_new


def _diff_block(qdm_ref, k_of, v_of, bias_of, mask, dm_ref, dacc_ref, first=False):
    for h in range(N_DIFF_HEADS):
        k, v = k_of(h), v_of(h)
        v_aug = jnp.concatenate([v, jnp.ones_like(v)], axis=1)
        bias = bias_of(h)
        for which in range(2):
            idx = 2 * h + which
            s = lax.dot_general(qdm_ref[idx], k, _NT, preferred_element_type=jnp.float32) + bias
            if mask is not None:
                s = jnp.where(mask, s, NEG)
            _diff_update(idx, s, v_aug, dm_ref, dacc_ref, first)


def _sb_block(qsm_ref, k_of, v_of, tri, mask, lo_lanes, sc_ref, sacc_ref, first=False):
    for pair in range(N_SB_HEADS // 2):
        k, v = k_of(pair), v_of(pair)
        zero = jnp.zeros_like(v)
        v_split = jnp.concatenate([jnp.where(lo_lanes, v, zero), jnp.where(lo_lanes, zero, v)], axis=0)
        a_parts = []
        for half in range(2):
            idx = 2 * pair + half
            z = lax.dot_general(qsm_ref[idx], k, _NT, preferred_element_type=jnp.float32)
            sp = _softplus_log2(z)
            if mask is not None:
                sp = jnp.where(mask, sp, 0.0)
            cum = jnp.dot(sp.astype(jnp.bfloat16), tri, preferred_element_type=jnp.float32)
            row_sum = jnp.sum(sp, axis=-1, keepdims=True)
            if first:
                log_a = [zc - cc for zc, cc in zip(_lane_chunks(z), _lane_chunks(cum))]
                sc_ref[idx] = jnp.broadcast_to(row_sum, sc_ref.shape[1:])
            else:
                c_old = sc_ref[idx]
                log_a = [zc - cc - c_old for zc, cc in zip(_lane_chunks(z), _lane_chunks(cum))]
                sc_ref[idx] = c_old + row_sum
            a = jnp.concatenate([jnp.exp2(la) for la in log_a], axis=1)
            if mask is not None:
                a = jnp.where(mask, a, 0.0)
            a_parts.append(a.astype(jnp.bfloat16))
        o = jnp.dot(jnp.concatenate(a_parts, axis=1), v_split, preferred_element_type=jnp.float32)
        if first:
            sacc_ref[pair] = o
        else:
            sacc_ref[pair] += o


def _attn_body(lam_ref, gd_ref, gs_ref, qd_ref, qs_ref, kd_ref, vd_ref, ks_ref, vs_ref,
               kdm_ref, vdm_ref, ksm_ref, vsm_ref, o_ref,
               qdm_ref, qsm_ref, dm_ref, dacc_ref, sc_ref, sacc_ref, tri_ref, *, tq, meta_only):
    j = 0 if meta_only else pl.program_id(1)
    lam = _lambda_value(lam_ref)
    lane = lax.broadcasted_iota(jnp.int32, (1, LANES), 1)
    lo_lanes = lane < DIFF_QK_DIM
    tri_r = lax.broadcasted_iota(jnp.int32, (TK, TK), 0)
    tri_c = lax.broadcasted_iota(jnp.int32, (TK, TK), 1)
    tri_ref[...] = jnp.where(tri_r >= tri_c, 1.0, 0.0).astype(jnp.bfloat16)

    for src, dst in ((qd_ref, qdm_ref), (qs_ref, qsm_ref)):
        for h in range(4):
            q = src[0, :, h * LANES:(h + 1) * LANES]
            zero = jnp.zeros_like(q)
            dst[2 * h] = jnp.where(lo_lanes, q, zero)
            dst[2 * h + 1] = jnp.where(lo_lanes, zero, q)

    q0 = 0 if meta_only else N_META + j * tq
    meta_pos = lax.broadcasted_iota(jnp.int32, (1, META_ROWS), 1)
    blk_pos = lax.broadcasted_iota(jnp.int32, (1, TK), 1)
    mrow = lax.broadcasted_iota(jnp.int32, (tq, META_ROWS), 0)
    mcol = lax.broadcasted_iota(jnp.int32, (tq, META_ROWS), 1)
    meta_valid = mcol < N_META
    head_cols = lambda h: slice(h * LANES, (h + 1) * LANES)

    def real_block(ref, start):
        return lambda h: ref[0, pl.ds(start, TK), head_cols(h)]

    def meta_block(ref):
        return lambda h: ref[:, head_cols(h)]

    def diff_meta(mask):
        _diff_block(qdm_ref, meta_block(kdm_ref), meta_block(vdm_ref),
                    lambda h: _alibi_slope(h) * LOG2E * (meta_pos - q0).astype(jnp.float32),
                    mask, dm_ref, dacc_ref, first=True)

    def sb_meta(mask, first):
        _sb_block(qsm_ref, meta_block(ksm_ref), meta_block(vsm_ref),
                  tri_ref[0:META_ROWS, 0:META_ROWS], mask, lo_lanes, sc_ref, sacc_ref, first=first)

    if meta_only:
        diff_meta(meta_valid & (mcol <= mrow))
        sb_meta(meta_valid & (mcol < mrow), first=True)
    else:
        row = lax.broadcasted_iota(jnp.int32, (tq, TK), 0)
        col = lax.broadcasted_iota(jnp.int32, (tq, TK), 1)
        diag = pl.multiple_of(j * TK, TK)
        diff_meta(meta_valid)
        _sb_block(qsm_ref, real_block(ks_ref, diag), real_block(vs_ref, diag),
                  tri_ref[...], col < row, lo_lanes, sc_ref, sacc_ref, first=True)

        def mid(t, carry):
            d_start = pl.multiple_of(t * TK, TK)
            s_start = pl.multiple_of((j - 1 - t) * TK, TK)
            _diff_block(qdm_ref, real_block(kd_ref, d_start), real_block(vd_ref, d_start),
                        lambda h: _alibi_slope(h) * LOG2E * ((t - j) * TK + blk_pos).astype(jnp.float32),
                        None, dm_ref, dacc_ref)
            _sb_block(qsm_ref, real_block(ks_ref, s_start), real_block(vs_ref, s_start),
                      tri_ref[...], None, lo_lanes, sc_ref, sacc_ref)
            return carry
        lax.fori_loop(0, j, mid, 0)

        _diff_block(qdm_ref, real_block(kd_ref, diag), real_block(vd_ref, diag),
                    lambda h: _alibi_slope(h) * LOG2E * blk_pos.astype(jnp.float32),
                    col <= row, dm_ref, dacc_ref)
        sb_meta(meta_valid, first=False)

    for h in range(N_DIFF_HEADS):
        o = (dacc_ref[2 * h, :, 0:LANES] / dacc_ref[2 * h, :, LANES:2 * LANES]
             - lam * (dacc_ref[2 * h + 1, :, 0:LANES] / dacc_ref[2 * h + 1, :, LANES:2 * LANES]))
        o = o * _rms_scale(o) * gd_ref[...] * (1.0 - LAMBDA_INIT)
        o_ref[0, :, head_cols(h)] = o.astype(o_ref.dtype)
    for pair in range(N_SB_HEADS // 2):
        o = sacc_ref[pair]
        sq = o * o
        ms_lo = jnp.sum(jnp.where(lo_lanes, sq, 0.0), axis=-1, keepdims=True)
        ms_hi = jnp.sum(jnp.where(lo_lanes, 0.0, sq), axis=-1, keepdims=True)
        ms = jnp.where(lo_lanes, ms_lo, ms_hi) * (1.0 / SB_HEAD_DIM)
        o = o * lax.rsqrt(ms + EPS) * gs_ref[...]
        o_ref[0, :, GROUP_WIDTH + pair * LANES:GROUP_WIDTH + (pair + 1) * LANES] = o.astype(o_ref.dtype)


def _attn_scratch(tq):
    maps = 2 * N_DIFF_HEADS
    return [pltpu.VMEM((maps, tq, LANES), jnp.bfloat16),
            pltpu.VMEM((N_SB_HEADS, tq, LANES), jnp.bfloat16),
            pltpu.VMEM((maps, tq, LANES), jnp.float32),
            pltpu.VMEM((maps, tq, 2 * LANES), jnp.float32),
            pltpu.VMEM((N_SB_HEADS, tq, LANES), jnp.float32),
            pltpu.VMEM((N_SB_HEADS // 2, tq, LANES), jnp.float32),
            pltpu.VMEM((TK, TK), jnp.bfloat16)]


def _meta_kv_specs(index_map_for):
    return [pl.BlockSpec((META_ROWS, GROUP_WIDTH), index_map_for(g)) for g in (1, 2, 4, 5)]


def _attn(proj, proj_meta, lam, gd, gs, batch, seq):
    proj3 = proj.reshape(batch, seq, PROJ_WIDTH)
    small = [pl.BlockSpec(lam.shape, lambda b, j: (0, 0)),
             pl.BlockSpec(gd.shape, lambda b, j: (0, 0)),
             pl.BlockSpec(gs.shape, lambda b, j: (0, 0))]
    qspec = lambda g: pl.BlockSpec((1, TQ, GROUP_WIDTH), lambda b, j, g=g: (b, j, g))
    kvspec = lambda g: pl.BlockSpec((1, seq, GROUP_WIDTH), lambda b, j, g=g: (b, 0, g))
    meta_specs = _meta_kv_specs(lambda g: (lambda b, j, g=g: (0, g)))
    out = pl.pallas_call(
        functools.partial(_attn_body, tq=TQ, meta_only=False),
        grid=(batch, seq // TQ),
        in_specs=small + [qspec(0), qspec(3), kvspec(1), kvspec(2), kvspec(4), kvspec(5)] + meta_specs,
        out_specs=pl.BlockSpec((1, TQ, D_MODEL), lambda b, j: (b, j, 0)),
        out_shape=jax.ShapeDtypeStruct((batch, seq, D_MODEL), jnp.bfloat16),
        scratch_shapes=_attn_scratch(TQ),
        compiler_params=_params(2),
        name="attn",
    )(lam, gd, gs, proj3, proj3, proj3, proj3, proj3, proj3,
      proj_meta, proj_meta, proj_meta, proj_meta)
    return out.reshape(batch * seq, D_MODEL)


def _attn_meta_body(lam_ref, gd_ref, gs_ref, qd_ref, qs_ref, kdm_ref, vdm_ref, ksm_ref, vsm_ref,
                    o_ref, *scratch):
    _attn_body(lam_ref, gd_ref, gs_ref, qd_ref, qs_ref, None, None, None, None,
               kdm_ref, vdm_ref, ksm_ref, vsm_ref, o_ref, *scratch,
               tq=META_ROWS, meta_only=True)


def _attn_meta(proj_meta, lam, gd, gs):
    proj3 = proj_meta.reshape(1, META_ROWS, PROJ_WIDTH)
    small = [pl.BlockSpec(lam.shape, lambda i: (0, 0)),
             pl.BlockSpec(gd.shape, lambda i: (0, 0)),
             pl.BlockSpec(gs.shape, lambda i: (0, 0))]
    qspec = lambda g: pl.BlockSpec((1, META_ROWS, GROUP_WIDTH), lambda i, g=g: (0, 0, g))
    meta_specs = _meta_kv_specs(lambda g: (lambda i, g=g: (0, g)))
    out = pl.pallas_call(
        _attn_meta_body,
        grid=(1,),
        in_specs=small + [qspec(0), qspec(3)] + meta_specs,
        out_specs=pl.BlockSpec((1, META_ROWS, D_MODEL), lambda i: (0, 0, 0)),
        out_shape=jax.ShapeDtypeStruct((1, META_ROWS, D_MODEL), jnp.bfloat16),
        scratch_shapes=_attn_scratch(META_ROWS),
        compiler_params=_params(1),
        name="attn_meta",
    )(lam, gd, gs, proj3, proj3, proj_meta, proj_meta, proj_meta, proj_meta)
    return out.reshape(META_ROWS, D_MODEL)


def _outproj_kernel(x_ref, a_ref, w_ref, o_ref):
    o_ref[...] = x_ref[...] + jnp.dot(a_ref[...], w_ref[...], preferred_element_type=jnp.float32)


def _outproj(x_rows, merged, w, tm):
    n = x_rows.shape[0]
    return pl.pallas_call(
        _outproj_kernel,
        grid=(n // tm,),
        in_specs=[pl.BlockSpec((tm, D_MODEL), lambda i: (i, 0)),
                  pl.BlockSpec((tm, D_MODEL), lambda i: (i, 0)),
                  pl.BlockSpec((D_MODEL, D_MODEL), lambda i: (0, 0))],
        out_specs=pl.BlockSpec((tm, D_MODEL), lambda i: (i, 0)),
        out_shape=jax.ShapeDtypeStruct((n, D_MODEL), jnp.float32),
        compiler_params=_params(1),
        name="outproj",
    )(x_rows, merged, w)


def _meta_up_kernel(h_ref, g_ref, w_ref, o_ref):
    h = h_ref[...]
    xn = (h * _rms_scale(h) * g_ref[...]).astype(jnp.bfloat16)
    for c in range(2 * D_FF // FF_CHUNK):
        cols = slice(c * FF_CHUNK, (c + 1) * FF_CHUNK)
        o_ref[:, cols] = jnp.dot(xn, w_ref[:, cols], preferred_element_type=jnp.float32)


def _meta_up(h_meta, g, w):
    return pl.pallas_call(
        _meta_up_kernel,
        grid=(1,),
        in_specs=[pl.BlockSpec((META_ROWS, D_MODEL), lambda i: (0, 0)),
                  pl.BlockSpec((1, D_MODEL), lambda i: (0, 0)),
                  pl.BlockSpec((D_MODEL, 2 * D_FF), lambda i: (0, 0))],
        out_specs=pl.BlockSpec((META_ROWS, 2 * D_FF), lambda i: (0, 0)),
        out_shape=jax.ShapeDtypeStruct((META_ROWS, 2 * D_FF), jnp.float32),
        compiler_params=_params(1),
        name="meta_up",
    )(h_meta, g, w)


def _ffn_kernel(x_ref, a_ref, wo_ref, g_ref, wu_ref, cw_ref, cb_ref, halo0_ref, wd_ref, gf_ref, o_ref,
                halo_ref, act_ref, *, tiles_per_seq):
    i = pl.program_id(0)

    @pl.when(i % tiles_per_seq == 0)
    def _():
        halo_ref[...] = halo0_ref[...]

    tm = x_ref.shape[0]
    h1 = x_ref[...] + jnp.dot(a_ref[...], wo_ref[...], preferred_element_type=jnp.float32)
    xn = (h1 * _rms_scale(h1) * g_ref[...]).astype(jnp.bfloat16)
    row = lax.broadcasted_iota(jnp.int32, (tm, FF_CHUNK), 0)

    def conv(cols):
        u = jnp.dot(xn, wu_ref[:, cols], preferred_element_type=jnp.float32)
        prev2 = halo_ref[HALO_ROWS - 2:HALO_ROWS - 1, cols]
        prev1 = halo_ref[HALO_ROWS - 1:HALO_ROWS, cols]
        u1 = jnp.where(row == 0, prev1, pltpu.roll(u, 1, 0))
        u2 = jnp.where(row == 0, prev2, jnp.where(row == 1, prev1, pltpu.roll(u, 2, 0)))
        halo_ref[:, cols] = u[tm - HALO_ROWS:, :]
        return cb_ref[:, cols] + cw_ref[0:1, cols] * u2 + cw_ref[1:2, cols] * u1 + cw_ref[2:3, cols] * u

    for c in range(D_FF // FF_CHUNK):
        gate = conv(slice(c * FF_CHUNK, (c + 1) * FF_CHUNK))
        up = conv(slice(D_FF + c * FF_CHUNK, D_FF + (c + 1) * FF_CHUNK))
        act = gate * (1.0 / (1.0 + jnp.exp2(gate * -LOG2E))) * up
        act_ref[:, c * FF_CHUNK:(c + 1) * FF_CHUNK] = act.astype(act_ref.dtype)
    y = h1 + jnp.dot(act_ref[...], wd_ref[...], preferred_element_type=jnp.float32)
    o_ref[...] = y * _rms_scale(y) * gf_ref[...]


def _ffn(x_rows, merged, wo, g, wu, cw, cb, halo0, wd, gf, tm, seq):
    n = x_rows.shape[0]
    const = lambda shape: pl.BlockSpec(shape, lambda i: (0, 0))
    return pl.pallas_call(
        functools.partial(_ffn_kernel, tiles_per_seq=seq // tm),
        grid=(n // tm,),
        in_specs=[pl.BlockSpec((tm, D_MODEL), lambda i: (i, 0)),
                  pl.BlockSpec((tm, D_MODEL), lambda i: (i, 0)),
                  const((D_MODEL, D_MODEL)),
                  const((1, D_MODEL)),
                  const((D_MODEL, 2 * D_FF)),
                  const((3, 2 * D_FF)),
                  const((1, 2 * D_FF)),
                  const((HALO_ROWS, 2 * D_FF)),
                  const((D_FF, D_MODEL)),
                  const((1, D_MODEL))],
        out_specs=pl.BlockSpec((tm, D_MODEL), lambda i: (i, 0)),
        out_shape=jax.ShapeDtypeStruct((n, D_MODEL), jnp.float32),
        scratch_shapes=[pltpu.VMEM((HALO_ROWS, 2 * D_FF), jnp.float32),
                        pltpu.VMEM((tm, D_FF), jnp.bfloat16)],
        compiler_params=_params(1),
        name="ffn",
    )(x_rows, merged, wo, g, wu, cw, cb, halo0, wd, gf)


def kernel(x, meta_tokens, g_attn, w_in, lam_q1, lam_k1, lam_q2, lam_k2, g_diff, g_sb, w_out, g_ffn,
           w_up, conv_w, conv_b, w_down, g_final):
    batch, seq, _ = x.shape
    bf16 = jnp.bfloat16
    qscale = jnp.concatenate([jnp.full((GROUP_WIDTH,), DIFF_QK_DIM ** -0.5 * LOG2E, jnp.float32),
                              jnp.ones((2 * GROUP_WIDTH,), jnp.float32),
                              jnp.full((GROUP_WIDTH,), SB_HEAD_DIM ** -0.5 * LOG2E, jnp.float32),
                              jnp.ones((2 * GROUP_WIDTH,), jnp.float32)])
    w_in_b = (w_in[0] * qscale).astype(bf16)
    w_out_b = w_out[0].astype(bf16)
    w_up_b = w_up[0].astype(bf16)
    w_down_b = w_down[0].astype(bf16)
    lam = jnp.stack([lam_q1[0], lam_k1[0], lam_q2[0], lam_k2[0]]).astype(jnp.float32)
    gd = g_diff[0].reshape(1, DIFF_V_DIM)
    gs = jnp.tile(g_sb[0], 2).reshape(1, LANES)
    g_attn2, g_ffn2, g_final2 = g_attn[0].reshape(1, -1), g_ffn[0].reshape(1, -1), g_final.reshape(1, -1)

    x_rows = x.reshape(batch * seq, D_MODEL)
    meta_rows = jnp.pad(meta_tokens.astype(x.dtype), ((0, META_ROWS - N_META), (0, 0)))

    proj_meta = _inproj(meta_rows, g_attn2, w_in_b, META_ROWS)
    merged_meta = _attn_meta(proj_meta, lam, gd, gs)
    h1_meta = _outproj(meta_rows, merged_meta, w_out_b, META_ROWS)
    u_meta = _meta_up(h1_meta, g_ffn2, w_up_b)
    halo0 = jnp.pad(u_meta[N_META - 2:N_META], ((HALO_ROWS - 2, 0), (0, 0)))

    proj = _inproj(x_rows, g_attn2, w_in_b, 1024)
    merged = _attn(proj, proj_meta, lam, gd, gs, batch, seq)
    out = _ffn(x_rows, merged, w_out_b, g_ffn2, w_up_b, conv_w[0], conv_b[0].reshape(1, -1), halo0,
               w_down_b, g_final2, 512, seq)
    return out.reshape(batch, seq, D_MODEL)
```

```python
import functools
import math

import jax
import jax.numpy as jnp
from jax import lax
from jax.experimental import pallas as pl
from jax.experimental.pallas import tpu as pltpu

D_MODEL = 1024
N_META = 16
N_DIFF_HEADS = 4
DIFF_QK_DIM = 64
DIFF_V_DIM = 128
N_SB_HEADS = 8
SB_HEAD_DIM = 64
GROUP_WIDTH = 512
PROJ_WIDTH = 6 * GROUP_WIDTH
D_FF = 2816
EPS = 1e-6
LAMBDA_INIT = 0.8 - 0.6 * math.exp(-0.3 * 0)
NEG = -1e30
LOG2E = math.log2(math.e)
EXP2_CLAMP = 126.0

LANES = 128
META_ROWS = 128
TQ = 256
TK = 256
FF_CHUNK = 256
HALO_ROWS = 8

VMEM_LIMIT = 56 * 1024 * 1024

_NT = (((1,), (1,)), ((), ()))


def _alibi_slope(h):
    start = 2.0 ** (-8.0 / N_DIFF_HEADS)
    return start ** (h + 1)


def _params(n_grid):
    return pltpu.CompilerParams(dimension_semantics=("arbitrary",) * n_grid,
                                vmem_limit_bytes=VMEM_LIMIT)


def _rms_scale(x):
    return lax.rsqrt(jnp.mean(x * x, axis=-1, keepdims=True) + EPS)


def _inproj_kernel(x_ref, g_ref, w_ref, o_ref):
    x = x_ref[...]
    xn = (x * _rms_scale(x) * g_ref[...]).astype(jnp.bfloat16)
    for c in range(PROJ_WIDTH // GROUP_WIDTH):
        cols = slice(c * GROUP_WIDTH, (c + 1) * GROUP_WIDTH)
        o_ref[:, cols] = jnp.dot(xn, w_ref[:, cols],
                                 preferred_element_type=jnp.float32).astype(jnp.bfloat16)


def _inproj(rows, g, w, tm):
    n = rows.shape[0]
    return pl.pallas_call(
        _inproj_kernel,
        grid=(n // tm,),
        in_specs=[pl.BlockSpec((tm, D_MODEL), lambda i: (i, 0)),
                  pl.BlockSpec((1, D_MODEL), lambda i: (0, 0)),
                  pl.BlockSpec((D_MODEL, PROJ_WIDTH), lambda i: (0, 0))],
        out_specs=pl.BlockSpec((tm, PROJ_WIDTH), lambda i: (i, 0)),
        out_shape=jax.ShapeDtypeStruct((n, PROJ_WIDTH), jnp.bfloat16),
        compiler_params=_params(1),
        name="inproj",
    )(rows, g, w)


def _lambda_value(lam_ref):
    lam = lam_ref[...]
    s1 = jnp.sum(lam[0:1] * lam[1:2], axis=-1, keepdims=True)
    s2 = jnp.sum(lam[2:3] * lam[3:4], axis=-1, keepdims=True)
    return jnp.exp(s1) - jnp.exp(s2) + LAMBDA_INIT


def _softplus_log2(z):
    return jnp.maximum(z, jnp.log2(1.0 + jnp.exp2(jnp.minimum(z, EXP2_CLAMP))))


def _lane_chunks(x):
    return [x[:, c:c + LANES] for c in range(0, x.shape[1], LANES)]


def _diff_update(idx, s, v_aug, dm_ref, dacc_ref, first):
    row_max = jnp.max(s, axis=-1, keepdims=True)
    if first:
        m_new = jnp.broadcast_to(row_max, dm_ref.shape[1:])
    else:
        m_old = dm_ref[idx]
        m_new = jnp.maximum(m_old, row_max)
        alpha = jnp.exp2(m_old - m_new)
    p = jnp.concatenate([jnp.exp2(sc - m_new) for sc in _lane_chunks(s)], axis=1)
    pv = jnp.dot(p.astype(jnp.bfloat16), v_aug, preferred_element_type=jnp.float32)
    for c in (0, LANES):
        if first:
            dacc_ref[idx, :, c:c + LANES] = pv[:, c:c + LANES]
        else:
            dacc_ref[idx, :, c:c + LANES] = alpha * dacc_ref[idx, :, c:c + LANES] + pv[:, c:c + LANES]
    dm_ref[idx] = m_new


def _diff_block(qdm_ref, k_of, v_of, bias_of, mask, dm_ref, dacc_ref, first=False):
    for h in range(N_DIFF_HEADS):
        k, v = k_of(h), v_of(h)
        v_aug = jnp.concatenate([v, jnp.ones_like(v)], axis=1)
        bias = bias_of(h)
        for which in range(2):
            idx = 2 * h + which
            s = lax.dot_general(qdm_ref[idx], k, _NT, preferred_element_type=jnp.float32) + bias
            if mask is not None:
                s = jnp.where(mask, s, NEG)
            _diff_update(idx, s, v_aug, dm_ref, dacc_ref, first)


def _sb_block(qsm_ref, k_of, v_of, tri, mask, lo_lanes, sc_ref, sacc_ref, first=False):
    for pair in range(N_SB_HEADS // 2):
        k, v = k_of(pair), v_of(pair)
        zero = jnp.zeros_like(v)
        v_split = jnp.concatenate([jnp.where(lo_lanes, v, zero), jnp.where(lo_lanes, zero, v)], axis=0)
        a_parts = []
        for half in range(2):
            idx = 2 * pair + half
            z = lax.dot_general(qsm_ref[idx], k, _NT, preferred_element_type=jnp.float32)
            sp = _softplus_log2(z)
            if mask is not None:
                sp = jnp.where(mask, sp, 0.0)
            cum = jnp.dot(sp.astype(jnp.bfloat16), tri, preferred_element_type=jnp.float32)
            row_sum = jnp.sum(sp, axis=-1, keepdims=True)
            if first:
                log_a = [zc - cc for zc, cc in zip(_lane_chunks(z), _lane_chunks(cum))]
                sc_ref[idx] = jnp.broadcast_to(row_sum, sc_ref.shape[1:])
            else:
                c_old = sc_ref[idx]
                log_a = [zc - cc - c_old for zc, cc in zip(_lane_chunks(z), _lane_chunks(cum))]
                sc_ref[idx] = c_old + row_sum
            a = jnp.concatenate([jnp.exp2(la) for la in log_a], axis=1)
            if mask is not None:
                a = jnp.where(mask, a, 0.0)
            a_parts.append(a.astype(jnp.bfloat16))
        o = jnp.dot(jnp.concatenate(a_parts, axis=1), v_split, preferred_element_type=jnp.float32)
        if first:
            sacc_ref[pair] = o
        else:
            sacc_ref[pair] += o


def _attn_body(lam_ref, gd_ref, gs_ref, qd_ref, qs_ref, kd_ref, vd_ref, ks_ref, vs_ref,
               kdm_ref, vdm_ref, ksm_ref, vsm_ref, o_ref,
               qdm_ref, qsm_ref, dm_ref, dacc_ref, sc_ref, sacc_ref, tri_ref, *, tq, meta_only):
    j = 0 if meta_only else pl.program_id(1)
    lam = _lambda_value(lam_ref)
    lane = lax.broadcasted_iota(jnp.int32, (1, LANES), 1)
    lo_lanes = lane < DIFF_QK_DIM
    tri_r = lax.broadcasted_iota(jnp.int32, (TK, TK), 0)
    tri_c = lax.broadcasted_iota(jnp.int32, (TK, TK), 1)
    tri_ref[...] = jnp.where(tri_r >= tri_c, 1.0, 0.0).astype(jnp.bfloat16)

    for src, dst in ((qd_ref, qdm_ref), (qs_ref, qsm_ref)):
        for h in range(4):
            q = src[0, :, h * LANES:(h + 1) * LANES]
            zero = jnp.zeros_like(q)
            dst[2 * h] = jnp.where(lo_lanes, q, zero)
            dst[2 * h + 1] = jnp.where(lo_lanes, zero, q)

    q0 = 0 if meta_only else N_META + j * tq
    meta_pos = lax.broadcasted_iota(jnp.int32, (1, META_ROWS), 1)
    blk_pos = lax.broadcasted_iota(jnp.int32, (1, TK), 1)
    mrow = lax.broadcasted_iota(jnp.int32, (tq, META_ROWS), 0)
    mcol = lax.broadcasted_iota(jnp.int32, (tq, META_ROWS), 1)
    meta_valid = mcol < N_META
    head_cols = lambda h: slice(h * LANES, (h + 1) * LANES)

    def real_block(ref, start):
        return lambda h: ref[0, pl.ds(start, TK), head_cols(h)]

    def meta_block(ref):
        return lambda h: ref[:, head_cols(h)]

    def diff_meta(mask):
        _diff_block(qdm_ref, meta_block(kdm_ref), meta_block(vdm_ref),
                    lambda h: _alibi_slope(h) * LOG2E * (meta_pos - q0).astype(jnp.float32),
                    mask, dm_ref, dacc_ref, first=True)

    def sb_meta(mask, first):
        _sb_block(qsm_ref, meta_block(ksm_ref), meta_block(vsm_ref),
                  tri_ref[0:META_ROWS, 0:META_ROWS], mask, lo_lanes, sc_ref, sacc_ref, first=first)

    if meta_only:
        diff_meta(meta_valid & (mcol <= mrow))
        sb_meta(meta_valid & (mcol < mrow), first=True)
    else:
        row = lax.broadcasted_iota(jnp.int32, (tq, TK), 0)
        col = lax.broadcasted_iota(jnp.int32, (tq, TK), 1)
        diag = pl.multiple_of(j * TK, TK)
        diff_meta(meta_valid)
        _sb_block(qsm_ref, real_block(ks_ref, diag), real_block(vs_ref, diag),
                  tri_ref[...], col < row, lo_lanes, sc_ref, sacc_ref, first=True)

        def mid(t, carry):
            d_start = pl.multiple_of(t * TK, TK)
            s_start = pl.multiple_of((j - 1 - t) * TK, TK)
            _diff_block(qdm_ref, real_block(kd_ref, d_start), real_block(vd_ref, d_start),
                        lambda h: _alibi_slope(h) * LOG2E * ((t - j) * TK + blk_pos).astype(jnp.float32),
                        None, dm_ref, dacc_ref)
            _sb_block(qsm_ref, real_block(ks_ref, s_start), real_block(vs_ref, s_start),
                      tri_ref[...], None, lo_lanes, sc_ref, sacc_ref)
            return carry
        lax.fori_loop(0, j, mid, 0)

        _diff_block(qdm_ref, real_block(kd_ref, diag), real_block(vd_ref, diag),
                    lambda h: _alibi_slope(h) * LOG2E * blk_pos.astype(jnp.float32),
                    col <= row, dm_ref, dacc_ref)
        sb_meta(meta_valid, first=False)

    for h in range(N_DIFF_HEADS):
        o = (dacc_ref[2 * h, :, 0:LANES] / dacc_ref[2 * h, :, LANES:2 * LANES]
             - lam * (dacc_ref[2 * h + 1, :, 0:LANES] / dacc_ref[2 * h + 1, :, LANES:2 * LANES]))
        o = o * _rms_scale(o) * gd_ref[...] * (1.0 - LAMBDA_INIT)
        o_ref[0, :, head_cols(h)] = o.astype(o_ref.dtype)
    for pair in range(N_SB_HEADS // 2):
        o = sacc_ref[pair]
        sq = o * o
        ms_lo = jnp.sum(jnp.where(lo_lanes, sq, 0.0), axis=-1, keepdims=True)
        ms_hi = jnp.sum(jnp.where(lo_lanes, 0.0, sq), axis=-1, keepdims=True)
        ms = jnp.where(lo_lanes, ms_lo, ms_hi) * (1.0 / SB_HEAD_DIM)
        o = o * lax.rsqrt(ms + EPS) * gs_ref[...]
        o_ref[0, :, GROUP_WIDTH + pair * LANES:GROUP_WIDTH + (pair + 1) * LANES] = o.astype(o_ref.dtype)


def _attn_scratch(tq):
    maps = 2 * N_DIFF_HEADS
    return [pltpu.VMEM((maps, tq, LANES), jnp.bfloat16),
            pltpu.VMEM((N_SB_HEADS, tq, LANES), jnp.bfloat16),
            pltpu.VMEM((maps, tq, LANES), jnp.float32),
            pltpu.VMEM((maps, tq, 2 * LANES), jnp.float32),
            pltpu.VMEM((N_SB_HEADS, tq, LANES), jnp.float32),
            pltpu.VMEM((N_SB_HEADS // 2, tq, LANES), jnp.float32),
            pltpu.VMEM((TK, TK), jnp.bfloat16)]


def _meta_kv_specs(index_map_for):
    return [pl.BlockSpec((META_ROWS, GROUP_WIDTH), index_map_for(g)) for g in (1, 2, 4, 5)]


def _attn(proj, proj_meta, lam, gd, gs, batch, seq):
    proj3 = proj.reshape(batch, seq, PROJ_WIDTH)
    small = [pl.BlockSpec(lam.shape, lambda b, j: (0, 0)),
             pl.BlockSpec(gd.shape, lambda b, j: (0, 0)),
             pl.BlockSpec(gs.shape, lambda b, j: (0, 0))]
    qspec = lambda g: pl.BlockSpec((1, TQ, GROUP_WIDTH), lambda b, j, g=g: (b, j, g))
    kvspec = lambda g: pl.BlockSpec((1, seq, GROUP_WIDTH), lambda b, j, g=g: (b, 0, g))
    meta_specs = _meta_kv_specs(lambda g: (lambda b, j, g=g: (0, g)))
    out = pl.pallas_call(
        functools.partial(_attn_body, tq=TQ, meta_only=False),
        grid=(batch, seq // TQ),
        in_specs=small + [qspec(0), qspec(3), kvspec(1), kvspec(2), kvspec(4), kvspec(5)] + meta_specs,
        out_specs=pl.BlockSpec((1, TQ, D_MODEL), lambda b, j: (b, j, 0)),
        out_shape=jax.ShapeDtypeStruct((batch, seq, D_MODEL), jnp.bfloat16),
        scratch_shapes=_attn_scratch(TQ),
        compiler_params=_params(2),
        name="attn",
    )(lam, gd, gs, proj3, proj3, proj3, proj3, proj3, proj3,
      proj_meta, proj_meta, proj_meta, proj_meta)
    return out.reshape(batch * seq, D_MODEL)


def _attn_meta_body(lam_ref, gd_ref, gs_ref, qd_ref, qs_ref, kdm_ref, vdm_ref, ksm_ref, vsm_ref,
                    o_ref, *scratch):
    _attn_body(lam_ref, gd_ref, gs_ref, qd_ref, qs_ref, None, None, None, None,
               kdm_ref, vdm_ref, ksm_ref, vsm_ref, o_ref, *scratch,
               tq=META_ROWS, meta_only=True)


def _attn_meta(proj_meta, lam, gd, gs):
    proj3 = proj_meta.reshape(1, META_ROWS, PROJ_WIDTH)
    small = [pl.BlockSpec(lam.shape, lambda i: (0, 0)),
             pl.BlockSpec(gd.shape, lambda i: (0, 0)),
             pl.BlockSpec(gs.shape, lambda i: (0, 0))]
    qspec = lambda g: pl.BlockSpec((1, META_ROWS, GROUP_WIDTH), lambda i, g=g: (0, 0, g))
    meta_specs = _meta_kv_specs(lambda g: (lambda i, g=g: (0, g)))
    out = pl.pallas_call(
        _attn_meta_body,
        grid=(1,),
        in_specs=small + [qspec(0), qspec(3)] + meta_specs,
        out_specs=pl.BlockSpec((1, META_ROWS, D_MODEL), lambda i: (0, 0, 0)),
        out_shape=jax.ShapeDtypeStruct((1, META_ROWS, D_MODEL), jnp.bfloat16),
        scratch_shapes=_attn_scratch(META_ROWS),
        compiler_params=_params(1),
        name="attn_meta",
    )(lam, gd, gs, proj3, proj3, proj_meta, proj_meta, proj_meta, proj_meta)
    return out.reshape(META_ROWS, D_MODEL)


def _outproj_kernel(x_ref, a_ref, w_ref, o_ref):
    o_ref[...] = x_ref[...] + jnp.dot(a_ref[...], w_ref[...], preferred_element_type=jnp.float32)


def _outproj(x_rows, merged, w, tm):
    n = x_rows.shape[0]
    return pl.pallas_call(
        _outproj_kernel,
        grid=(n // tm,),
        in_specs=[pl.BlockSpec((tm, D_MODEL), lambda i: (i, 0)),
                  pl.BlockSpec((tm, D_MODEL), lambda i: (i, 0)),
                  pl.BlockSpec((D_MODEL, D_MODEL), lambda i: (0, 0))],
        out_specs=pl.BlockSpec((tm, D_MODEL), lambda i: (i, 0)),
        out_shape=jax.ShapeDtypeStruct((n, D_MODEL), jnp.float32),
        compiler_params=_params(1),
        name="outproj",
    )(x_rows, merged, w)


def _meta_up_kernel(h_ref, g_ref, w_ref, o_ref):
    h = h_ref[...]
    xn = (h * _rms_scale(h) * g_ref[...]).astype(jnp.bfloat16)
    for c in range(2 * D_FF // FF_CHUNK):
        cols = slice(c * FF_CHUNK, (c + 1) * FF_CHUNK)
        o_ref[:, cols] = jnp.dot(xn, w_ref[:, cols], preferred_element_type=jnp.float32)


def _meta_up(h_meta, g, w):
    return pl.pallas_call(
        _meta_up_kernel,
        grid=(1,),
        in_specs=[pl.BlockSpec((META_ROWS, D_MODEL), lambda i: (0, 0)),
                  pl.BlockSpec((1, D_MODEL), lambda i: (0, 0)),
                  pl.BlockSpec((D_MODEL, 2 * D_FF), lambda i: (0, 0))],
        out_specs=pl.BlockSpec((META_ROWS, 2 * D_FF), lambda i: (0, 0)),
        out_shape=jax.ShapeDtypeStruct((META_ROWS, 2 * D_FF), jnp.float32),
        compiler_params=_params(1),
        name="meta_up",
    )(h_meta, g, w)


def _ffn_kernel(x_ref, a_ref, wo_ref, g_ref, wu_ref, cw_ref, cb_ref, halo0_ref, wd_ref, gf_ref, o_ref,
                halo_ref, act_ref, *, tiles_per_seq):
    i = pl.program_id(0)

    @pl.when(i % tiles_per_seq == 0)
    def _():
        halo_ref[...] = halo0_ref[...]

    tm = x_ref.shape[0]
    h1 = x_ref[...] + jnp.dot(a_ref[...], wo_ref[...], preferred_element_type=jnp.float32)
    xn = (h1 * _rms_scale(h1) * g_ref[...]).astype(jnp.bfloat16)
    row = lax.broadcasted_iota(jnp.int32, (tm, FF_CHUNK), 0)

    def conv(cols):
        u = jnp.dot(xn, wu_ref[:, cols], preferred_element_type=jnp.float32)
        prev2 = halo_ref[HALO_ROWS - 2:HALO_ROWS - 1, cols]
        prev1 = halo_ref[HALO_ROWS - 1:HALO_ROWS, cols]
        u1 = jnp.where(row == 0, prev1, pltpu.roll(u, 1, 0))
        u2 = jnp.where(row == 0, prev2, jnp.where(row == 1, prev1, pltpu.roll(u, 2, 0)))
        halo_ref[:, cols] = u[tm - HALO_ROWS:, :]
        return cb_ref[:, cols] + cw_ref[0:1, cols] * u2 + cw_ref[1:2, cols] * u1 + cw_ref[2:3, cols] * u

    for c in range(D_FF // FF_CHUNK):
        gate = conv(slice(c * FF_CHUNK, (c + 1) * FF_CHUNK))
        up = conv(slice(D_FF + c * FF_CHUNK, D_FF + (c + 1) * FF_CHUNK))
        act = gate * (1.0 / (1.0 + jnp.exp2(gate * -LOG2E))) * up
        act_ref[:, c * FF_CHUNK:(c + 1) * FF_CHUNK] = act.astype(act_ref.dtype)
    y = h1 + jnp.dot(act_ref[...], wd_ref[...], preferred_element_type=jnp.float32)
    o_ref[...] = y * _rms_scale(y) * gf_ref[...]


def _ffn(x_rows, merged, wo, g, wu, cw, cb, halo0, wd, gf, tm, seq):
    n = x_rows.shape[0]
    const = lambda shape: pl.BlockSpec(shape, lambda i: (0, 0), pipeline_mode=pl.Buffered(1))
    return pl.pallas_call(
        functools.partial(_ffn_kernel, tiles_per_seq=seq // tm),
        grid=(n // tm,),
        in_specs=[pl.BlockSpec((tm, D_MODEL), lambda i: (i, 0)),
                  pl.BlockSpec((tm, D_MODEL), lambda i: (i, 0)),
                  const((D_MODEL, D_MODEL)),
                  const((1, D_MODEL)),
                  const((D_MODEL, 2 * D_FF)),
                  const((3, 2 * D_FF)),
                  const((1, 2 * D_FF)),
                  const((HALO_ROWS, 2 * D_FF)),
                  const((D_FF, D_MODEL)),
                  const((1, D_MODEL))],
        out_specs=pl.BlockSpec((tm, D_MODEL), lambda i: (i, 0)),
        out_shape=jax.ShapeDtypeStruct((n, D_MODEL), jnp.float32),
        scratch_shapes=[pltpu.VMEM((HALO_ROWS, 2 * D_FF), jnp.float32),
                        pltpu.VMEM((tm, D_FF), jnp.bfloat16)],
        compiler_params=_params(1),
        name="ffn",
    )(x_rows, merged, wo, g, wu, cw, cb, halo0, wd, gf)


def kernel(x, meta_tokens, g_attn, w_in, lam_q1, lam_k1, lam_q2, lam_k2, g_diff, g_sb, w_out, g_ffn,
           w_up, conv_w, conv_b, w_down, g_final):
    batch, seq, _ = x.shape
    bf16 = jnp.bfloat16
    qscale = jnp.concatenate([jnp.full((GROUP_WIDTH,), DIFF_QK_DIM ** -0.5 * LOG2E, jnp.float32),
                              jnp.ones((2 * GROUP_WIDTH,), jnp.float32),
                              jnp.full((GROUP_WIDTH,), SB_HEAD_DIM ** -0.5 * LOG2E, jnp.float32),
                              jnp.ones((2 * GROUP_WIDTH,), jnp.float32)])
    w_in_b = (w_in[0] * qscale).astype(bf16)
    w_out_b = w_out[0].astype(bf16)
    w_up_b = w_up[0].astype(bf16)
    w_down_b = w_down[0].astype(bf16)
    lam = jnp.stack([lam_q1[0], lam_k1[0], lam_q2[0], lam_k2[0]]).astype(jnp.float32)
    gd = g_diff[0].reshape(1, DIFF_V_DIM)
    gs = jnp.tile(g_sb[0], 2).reshape(1, LANES)
    g_attn2, g_ffn2, g_final2 = g_attn[0].reshape(1, -1), g_ffn[0].reshape(1, -1), g_final.reshape(1, -1)

    x_rows = x.reshape(batch * seq, D_MODEL)
    meta_rows = jnp.pad(meta_tokens.astype(x.dtype), ((0, META_ROWS - N_META), (0, 0)))

    proj_meta = _inproj(meta_rows, g_attn2, w_in_b, META_ROWS)
    merged_meta = _attn_meta(proj_meta, lam, gd, gs)
    h1_meta = _outproj(meta_rows, merged_meta, w_out_b, META_ROWS)
    u_meta = _meta_up(h1_meta, g_ffn2, w_up_b)
    halo0 = jnp.pad(u_meta[N_META - 2:N_META], ((HALO_ROWS - 2, 0), (0, 0)))

    proj = _inproj(x_rows, g_attn2, w_in_b, 1024)
    merged = _attn(proj, proj_meta, lam, gd, gs, batch, seq)
    out = _ffn(x_rows, merged, w_out_b, g_ffn2, w_up_b, conv_w[0], conv_b[0].reshape(1, -1), halo0,
               w_down_b, g_final2, 1024, seq)
    return out.reshape(batch, seq, D_MODEL)
```
